```python
import jax, jax.numpy as jnp
from jax import lax
import numpy as np

D_MODEL = 1024
BATCH = 16
SEQ = 2048
DEPTH = 4

GRID_W = 64
CTX_LEN = 256

N_MIXERS = 2
N_LRU_LAYERS = (DEPTH + 1) // 2
N_ATTN_LAYERS = DEPTH // 2

D_RNN = 1280
LRU_BLOCKS = 16
LRU_BLOCK = D_RNN // LRU_BLOCKS
CONV_W = 4
CONV_PAD = (2, 1)
LRU_C = 8.0

N_Q_HEADS = 8
N_KV_HEADS = 2
HEAD_DIM = 128
Q_PER_KV = N_Q_HEADS // N_KV_HEADS
Q_WIDTH = N_Q_HEADS * HEAD_DIM
KV_WIDTH = N_KV_HEADS * HEAD_DIM
AXIS_DIM = HEAD_DIM // 2
ROPE_THETA = 10000.0
Q_BLOCK = 128

PEER_HEADS = 8
N_KEYS = 128
N_EXPERTS = N_KEYS * N_KEYS
D_KEY = 256
D_KEY_HALF = D_KEY // 2
PEER_TOPK = 16
PEER_CHUNK = 128

DEEPNORM_ALPHA = (2 * DEPTH) ** 0.25
DEEPNORM_BETA = (8 * DEPTH) ** -0.25
LN_EPS = 1e-6
RMS_EPS = 1e-6

kernel_name = "hybrid_rglru_gqa_peer_dit"

F32 = jnp.float32


def _layer_norm(x, g, b):
    xf = x.astype(F32)
    mu = jnp.mean(xf, axis=-1, keepdims=True)
    var = jnp.mean(jnp.square(xf - mu), axis=-1, keepdims=True)
    return ((xf - mu) * lax.rsqrt(var + LN_EPS) * g + b).astype(x.dtype)


def _rms_norm(x, g):
    xf = x.astype(F32)
    return (xf * lax.rsqrt(jnp.mean(xf * xf, axis=-1, keepdims=True) + RMS_EPS) * g).astype(x.dtype)


def _post_norm(res, delta, g, b):
    return _layer_norm(DEEPNORM_ALPHA * res + delta, g, b)


def _modulation(cond, w, b):
    m = jax.nn.silu(cond) @ w + b
    return jnp.split(m[..., None, :], 6, axis=-1)


def _dwconv_centred(x, w, b):
    out = lax.conv_general_dilated(x, w[:, None, :].astype(x.dtype), window_strides=(1,), padding=(CONV_PAD,),
                                   dimension_numbers=('NWC', 'WIO', 'NWC'), feature_group_count=x.shape[-1])
    return out + b


def _rglru_coeffs(xr, gate_w, gate_b, lam):
    bsz, n = xr.shape[:2]
    xb = xr.reshape(bsz, n, LRU_BLOCKS, LRU_BLOCK)
    gates = jnp.einsum('blnd,nde->blne', xb, gate_w) + gate_b
    r = jax.nn.sigmoid(gates[..., :LRU_BLOCK].astype(F32)).reshape(bsz, n, D_RNN)
    i = jax.nn.sigmoid(gates[..., LRU_BLOCK:].astype(F32)).reshape(bsz, n, D_RNN)
    log_a = -LRU_C * r * jax.nn.softplus(-lam.astype(F32))
    a = jnp.exp(log_a)
    b = jnp.sqrt(-jnp.expm1(2.0 * log_a)) * i * xr.astype(F32)
    return a, b


def _linear_scan(a, b, h0):
    b = b.at[:, 0].add(a[:, 0] * h0)

    def combine(lhs, rhs):
        return lhs[0] * rhs[0], rhs[0] * lhs[1] + rhs[1]

    return lax.associative_scan(combine, (a, b), axis=1)[1]


def _rglru_mixer(h_ctx, h_lat, w_in, conv_w, conv_b, gate_w, gate_b, lam, w_out, need_ctx):
    def branches(h):
        gate, xr = jnp.split(h @ w_in, 2, axis=-1)
        return jax.nn.gelu(gate, approximate=False), _dwconv_centred(xr, conv_w, conv_b)

    gate_c, x_c = branches(h_ctx)
    gate_l, x_l = branches(h_lat)
    ys_c, ys_l = [], []
    for d in range(2):
        a_c, b_c = _rglru_coeffs(x_c, gate_w[d], gate_b[d], lam[d])
        a_l, b_l = _rglru_coeffs(x_l, gate_w[d], gate_b[d], lam[d])
        if d == 1:
            a_c, b_c, a_l, b_l = (jnp.flip(t, axis=1) for t in (a_c, b_c, a_l, b_l))
        h_c = _linear_scan(a_c, b_c, jnp.zeros_like(b_c[:, 0]))
        h_l = _linear_scan(a_l, b_l, h_c[:, -1])
        if d == 1:
            h_c, h_l = jnp.flip(h_c, axis=1), jnp.flip(h_l, axis=1)
        ys_c.append(h_c)
        ys_l.append(h_l)
    y_l = ((ys_l[0] + ys_l[1]).astype(h_lat.dtype) * gate_l) @ w_out
    y_c = (((ys_c[0] + ys_c[1]).astype(h_ctx.dtype) * gate_c) @ w_out) if need_ctx else None
    return y_c, y_l


def _axial_rope_tables(rows):
    row, col = jnp.meshgrid(jnp.arange(rows), jnp.arange(GRID_W), indexing='ij')
    pos = jnp.stack([row.reshape(-1), col.reshape(-1)], axis=-1).astype(F32)
    inv_freq = ROPE_THETA ** (-jnp.arange(0, AXIS_DIM, 2, dtype=F32) / AXIS_DIM)
    ang = pos[:, :, None] * inv_freq
    return jnp.cos(ang), jnp.sin(ang)


def _apply_axial_rope(x, cos, sin):
    bsz, n, nh, _ = x.shape
    xf = x.astype(F32).reshape(bsz, n, nh, 2, 2, AXIS_DIM // 2)
    x1, x2 = xf[..., 0, :], xf[..., 1, :]
    c, s = cos[None, :, None], sin[None, :, None]
    out = jnp.stack([x1 * c - x2 * s, x2 * c + x1 * s], axis=-2)
    return out.reshape(x.shape).astype(x.dtype)


def _sdpa(q, k, v):
    s = jnp.einsum('bqhgd,blhd->bhgql', q, k, preferred_element_type=F32) * (HEAD_DIM ** -0.5)
    p = jax.nn.softmax(s, axis=-1).astype(v.dtype)
    return jnp.einsum('bhgql,blhd->bqhgd', p, v)


def _attention_mixer(h_ctx, h_lat, w_qkv, q_gain, k_gain, w_out, cos, sin, need_ctx):
    def project(h):
        bsz, n = h.shape[:2]
        q, k, v = jnp.split(h @ w_qkv, [Q_WIDTH, Q_WIDTH + KV_WIDTH], axis=-1)
        q = _rms_norm(q.reshape(bsz, n, N_Q_HEADS, HEAD_DIM), q_gain)
        k = _rms_norm(k.reshape(bsz, n, N_KV_HEADS, HEAD_DIM), k_gain)
        return q, k, v.reshape(bsz, n, N_KV_HEADS, HEAD_DIM)

    q_c, k_c, v_c = project(h_ctx)
    q_l, k_l, v_l = project(h_lat)
    q_l = _apply_axial_rope(q_l, cos, sin)
    k_l = _apply_axial_rope(k_l, cos, sin)
    k_all = jnp.concatenate([k_l, k_c], axis=1)
    v_all = jnp.concatenate([v_l, v_c], axis=1)
    bsz, n = h_lat.shape[:2]
    n_blk = n // Q_BLOCK
    q_blocks = q_l.reshape(bsz, n_blk, Q_BLOCK, N_KV_HEADS, Q_PER_KV, HEAD_DIM).transpose(1, 0, 2, 3, 4, 5)
    o_l = lax.map(lambda qb: _sdpa(qb, k_all, v_all), q_blocks)
    o_l = o_l.transpose(1, 0, 2, 3, 4, 5).reshape(bsz, n, Q_WIDTH)
    y_l = o_l @ w_out
    if need_ctx:
        n_c = h_ctx.shape[1]
        o_c = _sdpa(q_c.reshape(bsz, n_c, N_KV_HEADS, Q_PER_KV, HEAD_DIM), k_c, v_c).reshape(bsz, n_c, Q_WIDTH)
        y_c = o_c @ w_out
    else:
        y_c = None
    return y_c, y_l


def _peer(h, w_q, sub_keys, u_tab, v_tab):
    def chunk_fn(hc):
        q = (hc @ w_q).reshape(hc.shape[0], PEER_HEADS, 2, D_KEY_HALF)
        s = jnp.einsum('chpd,pnd->chpn', q.astype(F32), sub_keys.astype(F32))
        s1, i1 = lax.top_k(s[:, :, 0], PEER_TOPK)
        s2, i2 = lax.top_k(s[:, :, 1], PEER_TOPK)
        cand_s = (s1[..., :, None] + s2[..., None, :]).reshape(hc.shape[0], PEER_HEADS, PEER_TOPK * PEER_TOPK)
        cand_i = (i1[..., :, None] * N_KEYS + i2[..., None, :]).reshape(hc.shape[0], PEER_HEADS, PEER_TOPK * PEER_TOPK)
        top_s, pos = lax.top_k(cand_s, PEER_TOPK)
        idx = jnp.take_along_axis(cand_i, pos, axis=-1)
        g = jax.nn.softmax(top_s, axis=-1)
        u = jnp.take(u_tab, idx, axis=0)
        act = jax.nn.gelu(jnp.einsum('chkd,cd->chk', u, hc).astype(F32), approximate=False)
        w = (g * act).astype(hc.dtype)
        return jnp.einsum('chk,chkd->cd', w, jnp.take(v_tab, idx, axis=0))

    out = lax.map(chunk_fn, h.reshape(-1, PEER_CHUNK, h.shape[-1]))
    return out.reshape(h.shape)


def setup_inputs(seed: int = 0) -> dict:
    key = jax.random.key(seed)
    ks = jax.random.split(key, 24)

    def nrm(k, shape, scale):
        return scale * jax.random.normal(k, shape, F32)

    u_lam = jax.random.uniform(ks[13], (N_LRU_LAYERS, 2, D_RNN), F32, 0.9, 0.999)
    a0 = u_lam ** (1.0 / LRU_C)
    return {
        'x': nrm(ks[0], (BATCH, SEQ, D_MODEL), 1.0),
        'c': nrm(ks[1], (BATCH, D_MODEL), 1.0),
        'ctx': nrm(ks[2], (BATCH, CTX_LEN, D_MODEL), 1.0),
        'c_ctx': nrm(ks[3], (D_MODEL,), 1.0),
        'mod_w': nrm(ks[4], (DEPTH, D_MODEL, 6 * D_MODEL), D_MODEL ** -0.5),
        'mod_b': nrm(ks[5], (DEPTH, 6 * D_MODEL), 0.02),
        'ln_g': 1.0 + nrm(ks[6], (DEPTH, 2, D_MODEL), 0.02),
        'ln_b': nrm(ks[7], (DEPTH, 2, D_MODEL), 0.02),
        'lru_w_in': nrm(ks[8], (N_LRU_LAYERS, D_MODEL, 2 * D_RNN), D_MODEL ** -0.5),
        'lru_conv_w': nrm(ks[9], (N_LRU_LAYERS, CONV_W, D_RNN), CONV_W ** -0.5),
        'lru_conv_b': nrm(ks[10], (N_LRU_LAYERS, D_RNN), 0.02),
        'lru_gate_w': nrm(ks[11], (N_LRU_LAYERS, 2, LRU_BLOCKS, LRU_BLOCK, 2 * LRU_BLOCK), LRU_BLOCK ** -0.5),
        'lru_gate_b': nrm(ks[12], (N_LRU_LAYERS, 2, LRU_BLOCKS, 2 * LRU_BLOCK), 0.02),
        'lru_lambda': jnp.log(a0) - jnp.log1p(-a0),
        'lru_w_out': nrm(ks[14], (N_LRU_LAYERS, D_RNN, D_MODEL), DEEPNORM_BETA * D_RNN ** -0.5),
        'attn_w_qkv': nrm(ks[15], (N_ATTN_LAYERS, D_MODEL, Q_WIDTH + 2 * KV_WIDTH), D_MODEL ** -0.5),
        'attn_q_gain': 1.0 + nrm(ks[16], (N_ATTN_LAYERS, HEAD_DIM), 0.02),
        'attn_k_gain': 1.0 + nrm(ks[17], (N_ATTN_LAYERS, HEAD_DIM), 0.02),
        'attn_w_out': nrm(ks[18], (N_ATTN_LAYERS, Q_WIDTH, D_MODEL), DEEPNORM_BETA * Q_WIDTH ** -0.5),
        'peer_w_q': nrm(ks[19], (DEPTH, D_MODEL, PEER_HEADS * D_KEY), D_MODEL ** -0.5),
        'peer_sub_keys': nrm(ks[20], (DEPTH, 2, N_KEYS, D_KEY_HALF), D_KEY_HALF ** -0.5),
        'peer_u': nrm(ks[21], (DEPTH, N_EXPERTS, D_MODEL), D_MODEL ** -0.5),
        'peer_v': nrm(ks[22], (DEPTH, N_EXPERTS, D_MODEL), DEEPNORM_BETA),
    }


def reference(x, c, ctx, c_ctx, mod_w, mod_b, ln_g, ln_b, lru_w_in, lru_conv_w, lru_conv_b, lru_gate_w,
              lru_gate_b, lru_lambda, lru_w_out, attn_w_qkv, attn_q_gain, attn_k_gain, attn_w_out,
              peer_w_q, peer_sub_keys, peer_u, peer_v):
    d = x.shape[-1]
    rows = x.shape[1] // GRID_W
    cos, sin = _axial_rope_tables(rows)
    for i in range(DEPTH):
        need_ctx = i < DEPTH - 1
        sh1, sc1, g1, sh2, sc2, g2 = _modulation(c, mod_w[i], mod_b[i])
        csh1, csc1, cg1, csh2, csc2, cg2 = _modulation(c_ctx, mod_w[i], mod_b[i])
        h_lat = x * (1.0 + sc1) + sh1
        h_ctx = ctx * (1.0 + csc1) + csh1
        j = i // N_MIXERS
        if i % N_MIXERS == 0:
            y_ctx, y_lat = _rglru_mixer(h_ctx, h_lat, lru_w_in[j], lru_conv_w[j], lru_conv_b[j], lru_gate_w[j],
                                        lru_gate_b[j], lru_lambda[j], lru_w_out[j], need_ctx)
        else:
            y_ctx, y_lat = _attention_mixer(h_ctx, h_lat, attn_w_qkv[j], attn_q_gain[j], attn_k_gain[j],
                                            attn_w_out[j], cos, sin, need_ctx)
        x = _post_norm(x, g1 * y_lat, ln_g[i, 0], ln_b[i, 0])
        f_lat = _peer((x * (1.0 + sc2) + sh2).reshape(-1, d), peer_w_q[i], peer_sub_keys[i], peer_u[i], peer_v[i])
        x = _post_norm(x, g2 * f_lat.reshape(x.shape), ln_g[i, 1], ln_b[i, 1])
        if need_ctx:
            ctx = _post_norm(ctx, cg1 * y_ctx, ln_g[i, 0], ln_b[i, 0])
            f_ctx = _peer((ctx * (1.0 + csc2) + csh2).reshape(-1, d), peer_w_q[i], peer_sub_keys[i], peer_u[i], peer_v[i])
            ctx = _post_norm(ctx, cg2 * f_ctx.reshape(ctx.shape), ln_g[i, 1], ln_b[i, 1])
    return x
```

```python
import functools

import jax
import jax.numpy as jnp
from jax import lax
from jax.experimental import pallas as pl
from jax.experimental.pallas import tpu as pltpu

F32 = jnp.float32
BF16 = jnp.bfloat16
I32 = jnp.int32

NB = 16
GRID_W = 64
CONV_W = 4
LRU_C = 8.0
N_Q_HEADS = 8
N_KV_HEADS = 2
HEAD_DIM = 128
Q_PER_KV = N_Q_HEADS // N_KV_HEADS
ROPE_THETA = 10000.0
PEER_HEADS = 8
N_KEYS = 128
PEER_TOPK = 16
N_SLOTS = PEER_HEADS * PEER_TOPK
LN_EPS = 1e-6
RMS_EPS = 1e-6
LANES = 128
VMEM_LIMIT = 56 * 1024 * 1024

TM = 256
TT = TM // NB
TQ = 256
TM_ACT = 1024
EXP_CHUNK = 256
VAL_CHUNK = 1024


def _cparams(sem):
    return pltpu.CompilerParams(dimension_semantics=sem, vmem_limit_bytes=VMEM_LIMIT)


def _gelu(x):
    return 0.5 * x * (1.0 + lax.erf(x * 0.7071067811865476))


def _modulate(x, sc, sh):
    rows, d = x.shape
    x3 = x.reshape(rows // NB, NB, d)
    return (x3 * (1.0 + sc)[None] + sh[None]).reshape(rows, d)


def _post_norm(x, delta, gate, lg, lb, alpha):
    rows, d = x.shape
    v = alpha * x + (delta.reshape(rows // NB, NB, d) * gate[None]).reshape(rows, d)
    mu = jnp.mean(v, axis=-1, keepdims=True)
    vc = v - mu
    var = jnp.mean(vc * vc, axis=-1, keepdims=True)
    return vc * lax.rsqrt(var + LN_EPS) * lg + lb


def _mod_kernel(c_ref, w_ref, b_ref, o_ref):
    c = c_ref[...]
    s = (c * jax.nn.sigmoid(c)).astype(BF16)
    o_ref[...] = jnp.dot(s, w_ref[...].astype(BF16), preferred_element_type=F32) + b_ref[...]


def _modulation_table(cond, mod_w, mod_b):
    depth, d, d6 = mod_w.shape
    nblk = d6 // d
    return pl.pallas_call(
        _mod_kernel,
        grid=(depth, nblk),
        in_specs=[pl.BlockSpec((cond.shape[0], d), lambda l, j: (0, 0)),
                  pl.BlockSpec((None, d, d), lambda l, j: (l, 0, j)),
                  pl.BlockSpec((None, 1, d), lambda l, j: (l, 0, j))],
        out_specs=pl.BlockSpec((None, cond.shape[0], d), lambda l, j: (l, 0, j)),
        out_shape=jax.ShapeDtypeStruct((depth, cond.shape[0], d6), F32),
        compiler_params=_cparams(("arbitrary", "arbitrary")),
        name="modulation",
    )(cond, mod_w, mod_b.reshape(depth, 1, d6))


def _lru_in_kernel(x_ref, sc_ref, sh_ref, w_ref, gate_ref, xr_ref):
    h = _modulate(x_ref[...], sc_ref[...], sh_ref[...]).astype(BF16)
    u = jnp.dot(h, w_ref[...], preferred_element_type=F32)
    c = gate_ref.shape[-1]
    gate_ref[...] = _gelu(u[:, :c])
    xr_ref[...] = u[:, c:]


def _lru_in(x, sc, sh, w_in, n_lat_tiles):
    rows, d = x.shape
    c2 = w_in.shape[1]
    c = c2 // 2
    sel = lambda i: (i // n_lat_tiles, 0, 0)
    return pl.pallas_call(
        _lru_in_kernel,
        grid=(rows // TM,),
        in_specs=[pl.BlockSpec((TM, d), lambda i: (i, 0)),
                  pl.BlockSpec((None, NB, d), sel),
                  pl.BlockSpec((None, NB, d), sel),
                  pl.BlockSpec((d, c2), lambda i: (0, 0))],
        out_specs=[pl.BlockSpec((TM, c), lambda i: (i, 0)),
                   pl.BlockSpec((TM, c), lambda i: (i, 0))],
        out_shape=[jax.ShapeDtypeStruct((rows, c), F32), jax.ShapeDtypeStruct((rows, c), F32)],
        compiler_params=_cparams(("arbitrary",)),
        name="lru_in",
    )(x, sc, sh, w_in)


def _modmm_kernel(x_ref, sc_ref, sh_ref, w_ref, o_ref):
    h = _modulate(x_ref[...], sc_ref[...], sh_ref[...]).astype(BF16)
    o_ref[...] = jnp.dot(h, w_ref[...], preferred_element_type=F32)


def _modmm(x, sc, sh, w, n_lat_tiles, name):
    rows, d = x.shape
    n = w.shape[1]
    sel = lambda i: (i // n_lat_tiles, 0, 0)
    return pl.pallas_call(
        _modmm_kernel,
        grid=(rows // TM,),
        in_specs=[pl.BlockSpec((TM, d), lambda i: (i, 0)),
                  pl.BlockSpec((None, NB, d), sel),
                  pl.BlockSpec((None, NB, d), sel),
                  pl.BlockSpec((d, n), lambda i: (0, 0))],
        out_specs=pl.BlockSpec((TM, n), lambda i: (i, 0)),
        out_shape=jax.ShapeDtypeStruct((rows, n), F32),
        compiler_params=_cparams(("arbitrary",)),
        name=name,
    )(x, sc, sh, w)


def _rglru_coeffs(xc, wg_ref, gb_ref, sp_ref, a_ref, b_ref):
    c = xc.shape[-1]
    gates = jnp.dot(xc.astype(BF16), wg_ref[...], preferred_element_type=F32) + gb_ref[...]
    r = jax.nn.sigmoid(gates[:, :c])
    i = jax.nn.sigmoid(gates[:, c:])
    log_a = -LRU_C * r * sp_ref[...]
    a = jnp.exp(log_a)
    a_ref[...] = a
    b_ref[...] = jnp.sqrt(-(a * a + 1.0) * jnp.tanh(log_a)) * i * xc


def _lru_fwd_kernel(n_lat, n_ctx, xr_ref, prev_ref, next_ref, cw_ref, cb_ref, wg_ref, gb_ref, sp_ref,
                    xc_ref, hf_ref, a_ref, b_ref, h_ref):
    g = pl.program_id(0)

    @pl.when(g == 0)
    def _():
        h_ref[...] = jnp.zeros_like(h_ref)

    first = jnp.logical_or(g == 0, g == n_ctx)
    last = jnp.logical_or(g == n_ctx - 1, g == n_ctx + n_lat - 1)
    prev = jnp.where(first, 0.0, prev_ref[...])
    nxt = jnp.where(last, 0.0, next_ref[...])
    ext = jnp.concatenate([prev, xr_ref[...], nxt], axis=0)
    cw = cw_ref[...]
    xc = cb_ref[...] + cw[0:1] * ext[0:TM]
    for j in range(1, CONV_W):
        xc = xc + cw[j:j + 1] * ext[j * NB:j * NB + TM]
    xc_ref[...] = xc
    _rglru_coeffs(xc, wg_ref, gb_ref, sp_ref, a_ref, b_ref)
    h = h_ref[...]
    for t in range(TT):
        rows = pl.ds(t * NB, NB)
        h = a_ref[rows, :] * h + b_ref[rows, :]
        hf_ref[rows, :] = h
    h_ref[...] = h


def _lru_fwd(xr, conv_w, conv_b, wg, gb, sp, n_lat, n_ctx):
    rows, c = xr.shape
    halo_p, halo_n = 2 * NB, NB
    per_p, per_n = TM // halo_p, TM // halo_n
    chunk = lambda g: jnp.where(g < n_ctx, n_lat + g, g - n_ctx)
    const = lambda g: (0, 0)
    return pl.pallas_call(
        functools.partial(_lru_fwd_kernel, n_lat, n_ctx),
        grid=(n_lat + n_ctx,),
        in_specs=[pl.BlockSpec((TM, c), lambda g: (chunk(g), 0)),
                  pl.BlockSpec((halo_p, c), lambda g: (jnp.maximum(chunk(g) * per_p - 1, 0), 0)),
                  pl.BlockSpec((halo_n, c), lambda g: (jnp.minimum((chunk(g) + 1) * per_n, rows // halo_n - 1), 0)),
                  pl.BlockSpec((CONV_W, c), const),
                  pl.BlockSpec((1, c), const),
                  pl.BlockSpec((c, 2 * c), const),
                  pl.BlockSpec((1, 2 * c), const),
                  pl.BlockSpec((1, c), const)],
        out_specs=[pl.BlockSpec((TM, c), lambda g: (chunk(g), 0)),
                   pl.BlockSpec((TM, c), lambda g: (chunk(g), 0))],
        out_shape=[jax.ShapeDtypeStruct((rows, c), F32), jax.ShapeDtypeStruct((rows, c), F32)],
        scratch_shapes=[pltpu.VMEM((TM, c), F32), pltpu.VMEM((TM, c), F32), pltpu.VMEM((NB, c), F32)],
        compiler_params=_cparams(("arbitrary",)),
        name="lru_fwd",
    )(xr, xr, xr, conv_w, conv_b, wg, gb, sp)


def _lru_bwd_kernel(alpha, xc_ref, hf_ref, gate_ref, x_ref, g1_ref, wg_ref, gb_ref, sp_ref, wo_ref,
                    lg_ref, lb_ref, o_ref, a_ref, b_ref, y_ref, h_ref):
    g = pl.program_id(0)

    @pl.when(g == 0)
    def _():
        h_ref[...] = jnp.zeros_like(h_ref)

    _rglru_coeffs(xc_ref[...], wg_ref, gb_ref, sp_ref, a_ref, b_ref)
    h = h_ref[...]
    for t in range(TT - 1, -1, -1):
        rows = pl.ds(t * NB, NB)
        h = a_ref[rows, :] * h + b_ref[rows, :]
        y_ref[rows, :] = h
    h_ref[...] = h
    y = ((hf_ref[...] + y_ref[...]) * gate_ref[...]).astype(BF16)
    delta = jnp.dot(y, wo_ref[...], preferred_element_type=F32)
    o_ref[...] = _post_norm(x_ref[...], delta, g1_ref[...], lg_ref[...], lb_ref[...], alpha)


def _lru_bwd(xc, hf, gate, x, g1, wg, gb, sp, w_out, lg, lb, n_lat, n_ctx, alpha):
    rows, c = xc.shape
    d = x.shape[1]
    chunk = lambda g: jnp.where(g < n_ctx, n_lat + (n_ctx - 1 - g), n_lat - 1 - (g - n_ctx))
    tile = lambda g: (chunk(g), 0)
    const = lambda g: (0, 0)
    return pl.pallas_call(
        functools.partial(_lru_bwd_kernel, alpha),
        grid=(n_lat + n_ctx,),
        in_specs=[pl.BlockSpec((TM, c), tile),
                  pl.BlockSpec((TM, c), tile),
                  pl.BlockSpec((TM, c), tile),
                  pl.BlockSpec((TM, d), tile),
                  pl.BlockSpec((None, NB, d), lambda g: (jnp.where(g < n_ctx, 1, 0), 0, 0)),
                  pl.BlockSpec((c, 2 * c), const),
                  pl.BlockSpec((1, 2 * c), const),
                  pl.BlockSpec((1, c), const),
                  pl.BlockSpec((c, d), const),
                  pl.BlockSpec((1, d), const),
                  pl.BlockSpec((1, d), const)],
        out_specs=pl.BlockSpec((TM, d), tile),
        out_shape=jax.ShapeDtypeStruct((rows, d), F32),
        scratch_shapes=[pltpu.VMEM((TM, c), F32), pltpu.VMEM((TM, c), F32), pltpu.VMEM((TM, c), F32),
                        pltpu.VMEM((NB, c), F32)],
        compiler_params=_cparams(("arbitrary",)),
        name="lru_bwd",
    )(xc, hf, gate, x, g1, wg, gb, sp, w_out, lg, lb)


def _softplus_kernel(x_ref, o_ref):
    x = x_ref[...]
    o_ref[...] = jax.nn.softplus(x)


def _gate_weights(gate_w, gate_b):
    nblk, blk, _ = gate_w.shape
    c = nblk * blk
    eye = jnp.eye(nblk, dtype=gate_w.dtype)
    wr = jnp.einsum('nde,nm->ndme', gate_w[..., :blk], eye).reshape(c, c)
    wi = jnp.einsum('nde,nm->ndme', gate_w[..., blk:], eye).reshape(c, c)
    w = jnp.concatenate([wr, wi], axis=1).astype(BF16)
    b = jnp.concatenate([gate_b[:, :blk].reshape(1, c), gate_b[:, blk:].reshape(1, c)], axis=1)
    return w, b


def _qk_prep_kernel(qkv_ref, cos_ref, sa_ref, sb_ref, qg_ref, kg_ref, q_ref, k_ref, v_ref):
    cos, sa, sb = cos_ref[...], sa_ref[...], sb_ref[...]
    nq = N_Q_HEADS * HEAD_DIM
    nk = N_KV_HEADS * HEAD_DIM

    def norm_rope(xh, gain):
        ms = jnp.mean(xh * xh, axis=-1, keepdims=True)
        xn = xh * lax.rsqrt(ms + RMS_EPS) * gain
        return xn * cos + pltpu.roll(xn, 96, axis=1) * sa + pltpu.roll(xn, 32, axis=1) * sb

    for h in range(N_Q_HEADS):
        sl = slice(h * HEAD_DIM, (h + 1) * HEAD_DIM)
        q_ref[:, sl] = norm_rope(qkv_ref[:, sl], qg_ref[...]).astype(BF16)
    for h in range(N_KV_HEADS):
        sl = slice(h * HEAD_DIM, (h + 1) * HEAD_DIM)
        k_ref[:, sl] = norm_rope(qkv_ref[:, nq + h * HEAD_DIM:nq + (h + 1) * HEAD_DIM], kg_ref[...]).astype(BF16)
    v_ref[...] = qkv_ref[:, nq + nk:].astype(BF16)


def _qk_prep(qkv, cos, sa, sb, qg, kg, lt):
    w = qkv.shape[1]
    nq = N_Q_HEADS * HEAD_DIM
    nk = N_KV_HEADS * HEAD_DIM
    qkv2 = qkv.reshape(lt, NB * w)
    tl = TQ
    const = lambda b, i: (0, 0)
    return pl.pallas_call(
        _qk_prep_kernel,
        grid=(NB, lt // tl),
        in_specs=[pl.BlockSpec((tl, w), lambda b, i: (i, b)),
                  pl.BlockSpec((tl, HEAD_DIM), lambda b, i: (i, 0)),
                  pl.BlockSpec((tl, HEAD_DIM), lambda b, i: (i, 0)),
                  pl.BlockSpec((tl, HEAD_DIM), lambda b, i: (i, 0)),
                  pl.BlockSpec((1, HEAD_DIM), const),
                  pl.BlockSpec((1, HEAD_DIM), const)],
        out_specs=[pl.BlockSpec((None, tl, nq), lambda b, i: (b, i, 0)),
                   pl.BlockSpec((None, tl, nk), lambda b, i: (b, i, 0)),
                   pl.BlockSpec((None, tl, nk), lambda b, i: (b, i, 0))],
        out_shape=[jax.ShapeDtypeStruct((NB, lt, nq), BF16),
                   jax.ShapeDtypeStruct((NB, lt, nk), BF16),
                   jax.ShapeDtypeStruct((NB, lt, nk), BF16)],
        compiler_params=_cparams(("arbitrary", "arbitrary")),
        name="qk_prep",
    )(qkv2, cos, sa, sb, qg, kg)


def _attn_kernel(n_lat_tiles, seq, q_ref, k_ref, v_ref, o_ref):
    qi = pl.program_id(2)
    tq = q_ref.shape[0]
    q4 = jnp.concatenate([q_ref[:, j * HEAD_DIM:(j + 1) * HEAD_DIM] for j in range(Q_PER_KV)], axis=0)
    scale = HEAD_DIM ** -0.5

    def attend(k, v):
        s = lax.dot_general(q4, k, (((1,), (1,)), ((), ())), preferred_element_type=F32) * scale
        m = jnp.max(s, axis=-1, keepdims=True)
        p = jnp.exp(s - m)
        l = jnp.sum(p, axis=-1, keepdims=True)
        o = jnp.dot(p.astype(BF16), v, preferred_element_type=F32) / l
        for j in range(Q_PER_KV):
            o_ref[:, j * HEAD_DIM:(j + 1) * HEAD_DIM] = o[j * tq:(j + 1) * tq].astype(BF16)

    @pl.when(qi < n_lat_tiles)
    def _():
        attend(k_ref[...], v_ref[...])

    @pl.when(qi >= n_lat_tiles)
    def _():
        attend(k_ref[seq:, :], v_ref[seq:, :])


def _attention(q, k, v, seq):
    nb, lt, nq = q.shape
    tq = TQ
    gw = Q_PER_KV * HEAD_DIM
    return pl.pallas_call(
        functools.partial(_attn_kernel, seq // tq, seq),
        grid=(nb, N_KV_HEADS, lt // tq),
        in_specs=[pl.BlockSpec((None, tq, gw), lambda b, h, i: (b, i, h)),
                  pl.BlockSpec((None, lt, HEAD_DIM), lambda b, h, i: (b, 0, h)),
                  pl.BlockSpec((None, lt, HEAD_DIM), lambda b, h, i: (b, 0, h))],
        out_specs=pl.BlockSpec((tq, gw), lambda b, h, i: (i, b * N_KV_HEADS + h)),
        out_shape=jax.ShapeDtypeStruct((lt, nb * nq), BF16),
        compiler_params=_cparams(("arbitrary", "arbitrary", "arbitrary")),
        name="attention",
    )(q, k, v)


def _attn_out_kernel(alpha, o_ref, x_ref, g1_ref, w_ref, lg_ref, lb_ref, out_ref):
    delta = jnp.dot(o_ref[...], w_ref[...], preferred_element_type=F32)
    out_ref[...] = _post_norm(x_ref[...], delta, g1_ref[...], lg_ref[...], lb_ref[...], alpha)


def _attn_out(o, x, g1, w_out, lg, lb, n_lat_tiles, alpha):
    rows, d = x.shape
    nq = o.shape[1]
    const = lambda i: (0, 0)
    return pl.pallas_call(
        functools.partial(_attn_out_kernel, alpha),
        grid=(rows // TM,),
        in_specs=[pl.BlockSpec((TM, nq), lambda i: (i, 0)),
                  pl.BlockSpec((TM, d), lambda i: (i, 0)),
                  pl.BlockSpec((None, NB, d), lambda i: (i // n_lat_tiles, 0, 0)),
                  pl.BlockSpec((nq, d), const),
                  pl.BlockSpec((1, d), const),
                  pl.BlockSpec((1, d), const)],
        out_specs=pl.BlockSpec((TM, d), lambda i: (i, 0)),
        out_shape=jax.ShapeDtypeStruct((rows, d), F32),
        compiler_params=_cparams(("arbitrary",)),
        name="attn_out",
    )(o, x, g1, w_out, lg, lb)


def _rope_tables(seq, ctx_len):
    rows = seq // GRID_W
    row, col = jnp.meshgrid(jnp.arange(rows), jnp.arange(GRID_W), indexing='ij')
    pos = jnp.stack([row.reshape(-1), col.reshape(-1)], axis=-1).astype(F32)
    axis_dim = HEAD_DIM // 2
    inv_freq = ROPE_THETA ** (-jnp.arange(0, axis_dim, 2, dtype=F32) / axis_dim)
    ang = pos[:, :, None] * inv_freq
    cos, sin = jnp.cos(ang), jnp.sin(ang)
    zero = jnp.zeros_like(sin)
    cosf = jnp.concatenate([cos, cos], axis=-1).reshape(seq, HEAD_DIM)
    sa = jnp.concatenate([-sin, zero], axis=-1).reshape(seq, HEAD_DIM)
    sb = jnp.concatenate([zero, sin], axis=-1).reshape(seq, HEAD_DIM)
    pad = lambda t, v: jnp.concatenate([t, jnp.full((ctx_len, HEAD_DIM), v, F32)], axis=0)
    return pad(cosf, 1.0), pad(sa, 0.0), pad(sb, 0.0)


def _top16_rows(s, iota, n, val_ref, idx_ref, base):
    for r in range(PEER_TOPK):
        m = jnp.max(s, axis=0, keepdims=True)
        idx = jnp.min(jnp.where(s == m, iota, float(n)), axis=0, keepdims=True)
        s = jnp.where(iota == idx, -jnp.inf, s)
        val_ref[base + r:base + r + 1, :] = m
        idx_ref[base + r:base + r + 1, :] = idx


def _peer_route_kernel(x_ref, sc_ref, sh_ref, wq_ref, sk_ref, hb_ref, i1_ref, i2_ref, g_ref,
                       v1_ref, j1_ref, v2_ref, j2_ref, ts_ref, tp_ref, o1_ref, o2_ref, og_ref):
    h = _modulate(x_ref[...], sc_ref[...], sh_ref[...]).astype(BF16)
    hb_ref[...] = h
    hq = jnp.dot(h, wq_ref[...], preferred_element_type=F32).astype(BF16)
    tm = h.shape[0]
    key_iota = lax.broadcasted_iota(I32, (N_KEYS, tm), 0).astype(F32)
    cand_iota = lax.broadcasted_iota(I32, (PEER_TOPK * PEER_TOPK, tm), 0).astype(F32)
    nt = (((1,), (1,)), ((), ()))
    for hd in range(PEER_HEADS):
        for p, (v_ref, j_ref) in enumerate(((v1_ref, j1_ref), (v2_ref, j2_ref))):
            qs = hq[:, (hd * 2 + p) * LANES:(hd * 2 + p + 1) * LANES]
            s = lax.dot_general(sk_ref[p], qs, nt, preferred_element_type=F32)
            _top16_rows(s, key_iota, N_KEYS, v_ref, j_ref, 0)
        v2 = v2_ref[...]
        cand = jnp.concatenate([v1_ref[a:a + 1, :] + v2 for a in range(PEER_TOPK)], axis=0)
        _top16_rows(cand, cand_iota, PEER_TOPK * PEER_TOPK, ts_ref, tp_ref, 0)
        ts = ts_ref[...]
        pos = tp_ref[...].astype(I32)
        ak = jnp.right_shift(pos, 4)
        bk = jnp.bitwise_and(pos, PEER_TOPK - 1)
        i1 = jnp.zeros_like(ts)
        i2 = jnp.zeros_like(ts)
        for a in range(PEER_TOPK):
            i1 = jnp.where(ak == a, j1_ref[a:a + 1, :], i1)
            i2 = jnp.where(bk == a, j2_ref[a:a + 1, :], i2)
        e = jnp.exp(ts - ts[0:1, :])
        rows = slice(hd * PEER_TOPK, (hd + 1) * PEER_TOPK)
        o1_ref[rows, :] = i1
        o2_ref[rows, :] = i2
        og_ref[rows, :] = e / jnp.sum(e, axis=0, keepdims=True)
    i1_ref[...] = o1_ref[...].T.astype(I32)
    i2_ref[...] = o2_ref[...].T.astype(I32)
    g_ref[...] = og_ref[...].T


def _peer_route(x, sc, sh, w_q, sub_keys, n_lat_tiles):
    rows, d = x.shape
    nq = w_q.shape[1]
    sel = lambda i: (i // n_lat_tiles, 0, 0)
    tile = lambda i: (i, 0)
    k2 = PEER_TOPK * PEER_TOPK
    return pl.pallas_call(
        _peer_route_kernel,
        grid=(rows // TM,),
        in_specs=[pl.BlockSpec((TM, d), tile),
                  pl.BlockSpec((None, NB, d), sel),
                  pl.BlockSpec((None, NB, d), sel),
                  pl.BlockSpec((d, nq), lambda i: (0, 0)),
                  pl.BlockSpec(sub_keys.shape, lambda i: (0, 0, 0))],
        out_specs=[pl.BlockSpec((TM, d), tile),
                   pl.BlockSpec((TM, N_SLOTS), tile),
                   pl.BlockSpec((TM, N_SLOTS), tile),
                   pl.BlockSpec((TM, N_SLOTS), tile)],
        out_shape=[jax.ShapeDtypeStruct((rows, d), BF16),
                   jax.ShapeDtypeStruct((rows, N_SLOTS), I32),
                   jax.ShapeDtypeStruct((rows, N_SLOTS), I32),
                   jax.ShapeDtypeStruct((rows, N_SLOTS), F32)],
        scratch_shapes=[pltpu.VMEM((PEER_TOPK, TM), F32)] * 4
                       + [pltpu.VMEM((PEER_TOPK, TM), F32)] * 2
                       + [pltpu.VMEM((N_SLOTS, TM), F32)] * 3,
        compiler_params=_cparams(("arbitrary",)),
        name="peer_route",
    )(x, sc, sh, w_q, sub_keys)


def _peer_act_kernel(hb_ref, u_ref, i1_ref, i2_ref, g_ref, w_ref, z_ref):
    cj = pl.program_id(1)

    @pl.when(cj == 0)
    def _():
        z_ref[...] = jnp.zeros_like(z_ref)

    s = lax.dot_general(hb_ref[...], u_ref[...], (((1,), (1,)), ((), ())), preferred_element_type=F32)
    i1 = i1_ref[...]
    i2 = i2_ref[...]
    z = z_ref[...]
    for q in range(EXP_CHUNK // N_KEYS):
        picked = jnp.take_along_axis(s[:, q * N_KEYS:(q + 1) * N_KEYS], i2, axis=1)
        z = jnp.where(i1 == cj * (EXP_CHUNK // N_KEYS) + q, picked, z)
    z_ref[...] = z

    @pl.when(cj == pl.num_programs(1) - 1)
    def _():
        w_ref[...] = g_ref[...] * _gelu(z)


def _peer_act(hb, u_tab, i1, i2, g):
    rows, d = hb.shape
    n_exp = u_tab.shape[0]
    tile = lambda i, j: (i, 0)
    return pl.pallas_call(
        _peer_act_kernel,
        grid=(rows // TM_ACT, n_exp // EXP_CHUNK),
        in_specs=[pl.BlockSpec((TM_ACT, d), tile),
                  pl.BlockSpec((EXP_CHUNK, d), lambda i, j: (j, 0)),
                  pl.BlockSpec((TM_ACT, N_SLOTS), tile),
                  pl.BlockSpec((TM_ACT, N_SLOTS), tile),
                  pl.BlockSpec((TM_ACT, N_SLOTS), tile)],
        out_specs=pl.BlockSpec((TM_ACT, N_SLOTS), tile),
        out_shape=jax.ShapeDtypeStruct((rows, N_SLOTS), F32),
        scratch_shapes=[pltpu.VMEM((TM_ACT, N_SLOTS), F32)],
        compiler_params=_cparams(("arbitrary", "arbitrary")),
        name="peer_act",
    )(hb, u_tab, i1, i2, g)


def _peer_out_kernel(alpha, i1_ref, i2_ref, w_ref, v_ref, x_ref, g2_ref, lg_ref, lb_ref, o_ref,
                     wbuf_ref, acc_ref):
    vj = pl.program_id(1)
    tm = x_ref.shape[0]
    groups = VAL_CHUNK // N_KEYS

    @pl.when(vj == 0)
    def _():
        acc_ref[...] = jnp.zeros_like(acc_ref)
        key_iota = lax.broadcasted_iota(I32, (N_KEYS, N_SLOTS), 0)
        nt = (((1,), (1,)), ((), ()))

        def body(t, carry):
            row = pl.ds(t, 1)
            i1 = jnp.broadcast_to(i1_ref[row, :], (N_KEYS, N_SLOTS))
            i2 = jnp.broadcast_to(i2_ref[row, :], (N_KEYS, N_SLOTS))
            w = jnp.broadcast_to(w_ref[row, :], (N_KEYS, N_SLOTS))
            lhs = jnp.where(i1 == key_iota, w, 0.0).astype(BF16)
            rhs = jnp.where(i2 == key_iota, 1.0, 0.0).astype(BF16)
            wbuf_ref[pl.ds(pl.multiple_of(t * N_KEYS, N_KEYS), N_KEYS), :] = lax.dot_general(
                lhs, rhs, nt, preferred_element_type=F32)
            return carry

        lax.fori_loop(0, tm, body, 0)

    lhs = jnp.concatenate(
        [wbuf_ref[pl.ds(vj * groups + q, tm, stride=N_KEYS), :] for q in range(groups)], axis=1).astype(BF16)
    acc_ref[...] += jnp.dot(lhs, v_ref[...], preferred_element_type=F32)

    @pl.when(vj == pl.num_programs(1) - 1)
    def _():
        o_ref[...] = _post_norm(x_ref[...], acc_ref[...], g2_ref[...], lg_ref[...], lb_ref[...], alpha)


def _peer_out(i1, i2, w, v_tab, x, g2, lg, lb, n_lat_tiles, alpha):
    rows, d = x.shape
    n_exp = v_tab.shape[0]
    tile = lambda i, j: (i, 0)
    const = lambda i, j: (0, 0)
    return pl.pallas_call(
        functools.partial(_peer_out_kernel, alpha),
        grid=(rows // TM, n_exp // VAL_CHUNK),
        in_specs=[pl.BlockSpec((TM, N_SLOTS), tile),
                  pl.BlockSpec((TM, N_SLOTS), tile),
                  pl.BlockSpec((TM, N_SLOTS), tile),
                  pl.BlockSpec((VAL_CHUNK, d), lambda i, j: (j, 0)),
                  pl.BlockSpec((TM, d), tile),
                  pl.BlockSpec((None, NB, d), lambda i, j: (i // n_lat_tiles, 0, 0)),
                  pl.BlockSpec((1, d), const),
                  pl.BlockSpec((1, d), const)],
        out_specs=pl.BlockSpec((TM, d), tile),
        out_shape=jax.ShapeDtypeStruct((rows, d), F32),
        scratch_shapes=[pltpu.VMEM((TM * N_KEYS, N_KEYS), F32), pltpu.VMEM((TM, d), F32)],
        compiler_params=_cparams(("arbitrary", "arbitrary")),
        name="peer_out",
    )(i1, i2, w, v_tab, x, g2, lg, lb)


def kernel(x, c, ctx, c_ctx, mod_w, mod_b, ln_g, ln_b, lru_w_in, lru_conv_w, lru_conv_b, lru_gate_w, lru_gate_b,
           lru_lambda, lru_w_out, attn_w_qkv, attn_q_gain, attn_k_gain, attn_w_out, peer_w_q, peer_sub_keys,
           peer_u, peer_v):
    nb, seq, d = x.shape
    ctx_len = ctx.shape[1]
    depth = mod_w.shape[0]
    assert nb == NB and seq % GRID_W == 0 and seq % TQ == 0 and ctx_len % TQ == 0
    for t in (TM, TM_ACT):
        assert (seq * NB) % t == 0 and (ctx_len * NB) % t == 0
    lt = seq + ctx_len
    n_lat, n_ctx = seq * NB // TM, ctx_len * NB // TM
    alpha = (2 * depth) ** 0.25

    xs = jnp.concatenate([x.transpose(1, 0, 2), ctx.transpose(1, 0, 2)], axis=0).reshape(lt * NB, d)

    cond = jnp.concatenate([c, c_ctx[None], jnp.zeros((7, d), F32)], axis=0)
    mtab = _modulation_table(cond, mod_w, mod_b)
    lat = mtab[:, :NB].reshape(depth, NB, 6, d).transpose(0, 2, 1, 3)
    cx = jnp.broadcast_to(mtab[:, NB].reshape(depth, 6, 1, d), (depth, 6, NB, d))
    mods = jnp.stack([lat, cx], axis=2)

    cos, sa, sb = _rope_tables(seq, ctx_len)
    lam = lru_lambda.reshape(-1, lru_lambda.shape[-1])
    softplus_neg_lam = pl.pallas_call(
        _softplus_kernel, out_shape=jax.ShapeDtypeStruct(lam.shape, F32), name="softplus")(-lam)
    softplus_neg_lam = softplus_neg_lam.reshape(lru_lambda.shape)

    for i in range(depth):
        sh1, sc1, g1, sh2, sc2, g2 = (mods[i, j] for j in range(6))
        lg1, lb1 = ln_g[i, 0][None], ln_b[i, 0][None]
        lg2, lb2 = ln_g[i, 1][None], ln_b[i, 1][None]
        j = i // 2
        if i % 2 == 0:
            gate, xr = _lru_in(xs, sc1, sh1, lru_w_in[j].astype(BF16), n_lat)
            wg0, gb0 = _gate_weights(lru_gate_w[j, 0], lru_gate_b[j, 0])
            wg1, gb1 = _gate_weights(lru_gate_w[j, 1], lru_gate_b[j, 1])
            xc, hf = _lru_fwd(xr, lru_conv_w[j], lru_conv_b[j][None], wg0, gb0, softplus_neg_lam[j, 0][None],
                              n_lat, n_ctx)
            xs = _lru_bwd(xc, hf, gate, xs, g1, wg1, gb1, softplus_neg_lam[j, 1][None],
                          lru_w_out[j].astype(BF16), lg1, lb1, n_lat, n_ctx, alpha)
        else:
            qkv = _modmm(xs, sc1, sh1, attn_w_qkv[j].astype(BF16), n_lat, "qkv")
            q, k, v = _qk_prep(qkv, cos, sa, sb, attn_q_gain[j][None], attn_k_gain[j][None], lt)
            o = _attention(q, k, v, seq).reshape(lt * NB, N_Q_HEADS * HEAD_DIM)
            xs = _attn_out(o, xs, g1, attn_w_out[j].astype(BF16), lg1, lb1, n_lat, alpha)
        hb, i1, i2, g = _peer_route(xs, sc2, sh2, peer_w_q[i].astype(BF16), peer_sub_keys[i].astype(BF16), n_lat)
        w = _peer_act(hb, peer_u[i].astype(BF16), i1, i2, g)
        xs = _peer_out(i1, i2, w, peer_v[i].astype(BF16), xs, g2, lg2, lb2, n_lat, alpha)

    return xs.reshape(lt, NB, d)[:seq].transpose(1, 0, 2)
```

```python
import functools

import jax
import jax.numpy as jnp
from jax import lax
from jax.experimental import pallas as pl
from jax.experimental.pallas import tpu as pltpu

F32 = jnp.float32
BF16 = jnp.bfloat16
I32 = jnp.int32

NB = 16
GRID_W = 64
CONV_W = 4
LRU_C = 8.0
N_Q_HEADS = 8
N_KV_HEADS = 2
HEAD_DIM = 128
Q_PER_KV = N_Q_HEADS // N_KV_HEADS
ROPE_THETA = 10000.0
PEER_HEADS = 8
N_KEYS = 128
PEER_TOPK = 16
N_SLOTS = PEER_HEADS * PEER_TOPK
LN_EPS = 1e-6
RMS_EPS = 1e-6
LANES = 128
VMEM_LIMIT = 56 * 1024 * 1024

TM = 256
TT = TM // NB
TQ = 256
TM_ACT = 1024
EXP_CHUNK = 1024
VAL_CHUNK = 1024
W_STRIDE = N_KEYS + 8
ASM_UNROLL = 8


def _cparams(sem):
    return pltpu.CompilerParams(dimension_semantics=sem, vmem_limit_bytes=VMEM_LIMIT)


def _gelu(x):
    return 0.5 * x * (1.0 + lax.erf(x * 0.7071067811865476))


def _modulate(x, sc, sh):
    rows, d = x.shape
    x3 = x.reshape(rows // NB, NB, d)
    return (x3 * (1.0 + sc)[None] + sh[None]).reshape(rows, d)


def _post_norm(x, delta, gate, lg, lb, alpha):
    rows, d = x.shape
    v = alpha * x + (delta.reshape(rows // NB, NB, d) * gate[None]).reshape(rows, d)
    mu = jnp.mean(v, axis=-1, keepdims=True)
    vc = v - mu
    var = jnp.mean(vc * vc, axis=-1, keepdims=True)
    return vc * lax.rsqrt(var + LN_EPS) * lg + lb


def _mod_kernel(c_ref, w_ref, b_ref, o_ref):
    c = c_ref[...]
    s = (c * jax.nn.sigmoid(c)).astype(BF16)
    o_ref[...] = jnp.dot(s, w_ref[...].astype(BF16), preferred_element_type=F32) + b_ref[...]


def _modulation_table(cond, mod_w, mod_b):
    depth, d, d6 = mod_w.shape
    nblk = d6 // d
    return pl.pallas_call(
        _mod_kernel,
        grid=(depth, nblk),
        in_specs=[pl.BlockSpec((cond.shape[0], d), lambda l, j: (0, 0)),
                  pl.BlockSpec((None, d, d), lambda l, j: (l, 0, j)),
                  pl.BlockSpec((None, 1, d), lambda l, j: (l, 0, j))],
        out_specs=pl.BlockSpec((None, cond.shape[0], d), lambda l, j: (l, 0, j)),
        out_shape=jax.ShapeDtypeStruct((depth, cond.shape[0], d6), F32),
        compiler_params=_cparams(("arbitrary", "arbitrary")),
        name="modulation",
    )(cond, mod_w, mod_b.reshape(depth, 1, d6))


def _lru_in_kernel(x_ref, sc_ref, sh_ref, w_ref, gate_ref, xr_ref):
    h = _modulate(x_ref[...], sc_ref[...], sh_ref[...]).astype(BF16)
    u = jnp.dot(h, w_ref[...], preferred_element_type=F32)
    c = gate_ref.shape[-1]
    gate_ref[...] = _gelu(u[:, :c])
    xr_ref[...] = u[:, c:]


def _lru_in(x, sc, sh, w_in, n_lat_tiles):
    rows, d = x.shape
    c2 = w_in.shape[1]
    c = c2 // 2
    sel = lambda i: (i // n_lat_tiles, 0, 0)
    return pl.pallas_call(
        _lru_in_kernel,
        grid=(rows // TM,),
        in_specs=[pl.BlockSpec((TM, d), lambda i: (i, 0)),
                  pl.BlockSpec((None, NB, d), sel),
                  pl.BlockSpec((None, NB, d), sel),
                  pl.BlockSpec((d, c2), lambda i: (0, 0))],
        out_specs=[pl.BlockSpec((TM, c), lambda i: (i, 0)),
                   pl.BlockSpec((TM, c), lambda i: (i, 0))],
        out_shape=[jax.ShapeDtypeStruct((rows, c), F32), jax.ShapeDtypeStruct((rows, c), F32)],
        compiler_params=_cparams(("arbitrary",)),
        name="lru_in",
    )(x, sc, sh, w_in)


def _modmm_kernel(x_ref, sc_ref, sh_ref, w_ref, o_ref):
    h = _modulate(x_ref[...], sc_ref[...], sh_ref[...]).astype(BF16)
    o_ref[...] = jnp.dot(h, w_ref[...], preferred_element_type=F32)


def _modmm(x, sc, sh, w, n_lat_tiles, name):
    rows, d = x.shape
    n = w.shape[1]
    sel = lambda i: (i // n_lat_tiles, 0, 0)
    return pl.pallas_call(
        _modmm_kernel,
        grid=(rows // TM,),
        in_specs=[pl.BlockSpec((TM, d), lambda i: (i, 0)),
                  pl.BlockSpec((None, NB, d), sel),
                  pl.BlockSpec((None, NB, d), sel),
                  pl.BlockSpec((d, n), lambda i: (0, 0))],
        out_specs=pl.BlockSpec((TM, n), lambda i: (i, 0)),
        out_shape=jax.ShapeDtypeStruct((rows, n), F32),
        compiler_params=_cparams(("arbitrary",)),
        name=name,
    )(x, sc, sh, w)


def _rglru_coeffs(xc, wg_ref, gb_ref, sp_ref, a_ref, b_ref):
    c = xc.shape[-1]
    gates = jnp.dot(xc.astype(BF16), wg_ref[...], preferred_element_type=F32) + gb_ref[...]
    r = jax.nn.sigmoid(gates[:, :c])
    i = jax.nn.sigmoid(gates[:, c:])
    log_a = -LRU_C * r * sp_ref[...]
    a = jnp.exp(log_a)
    a_ref[...] = a
    b_ref[...] = jnp.sqrt(-(a * a + 1.0) * jnp.tanh(log_a)) * i * xc


def _lru_fwd_kernel(n_lat, n_ctx, xr_ref, prev_ref, next_ref, cw_ref, cb_ref, wg_ref, gb_ref, sp_ref,
                    xc_ref, hf_ref, a_ref, b_ref, h_ref):
    g = pl.program_id(0)

    @pl.when(g == 0)
    def _():
        h_ref[...] = jnp.zeros_like(h_ref)

    first = jnp.logical_or(g == 0, g == n_ctx)
    last = jnp.logical_or(g == n_ctx - 1, g == n_ctx + n_lat - 1)
    prev = jnp.where(first, 0.0, prev_ref[...])
    nxt = jnp.where(last, 0.0, next_ref[...])
    ext = jnp.concatenate([prev, xr_ref[...], nxt], axis=0)
    cw = cw_ref[...]
    xc = cb_ref[...] + cw[0:1] * ext[0:TM]
    for j in range(1, CONV_W):
        xc = xc + cw[j:j + 1] * ext[j * NB:j * NB + TM]
    xc_ref[...] = xc
    _rglru_coeffs(xc, wg_ref, gb_ref, sp_ref, a_ref, b_ref)
    h = h_ref[...]
    for t in range(TT):
        rows = pl.ds(t * NB, NB)
        h = a_ref[rows, :] * h + b_ref[rows, :]
        hf_ref[rows, :] = h
    h_ref[...] = h


def _lru_fwd(xr, conv_w, conv_b, wg, gb, sp, n_lat, n_ctx):
    rows, c = xr.shape
    halo_p, halo_n = 2 * NB, NB
    per_p, per_n = TM // halo_p, TM // halo_n
    chunk = lambda g: jnp.where(g < n_ctx, n_lat + g, g - n_ctx)
    const = lambda g: (0, 0)
    return pl.pallas_call(
        functools.partial(_lru_fwd_kernel, n_lat, n_ctx),
        grid=(n_lat + n_ctx,),
        in_specs=[pl.BlockSpec((TM, c), lambda g: (chunk(g), 0)),
                  pl.BlockSpec((halo_p, c), lambda g: (jnp.maximum(chunk(g) * per_p - 1, 0), 0)),
                  pl.BlockSpec((halo_n, c), lambda g: (jnp.minimum((chunk(g) + 1) * per_n, rows // halo_n - 1), 0)),
                  pl.BlockSpec((CONV_W, c), const),
                  pl.BlockSpec((1, c), const),
                  pl.BlockSpec((c, 2 * c), const),
                  pl.BlockSpec((1, 2 * c), const),
                  pl.BlockSpec((1, c), const)],
        out_specs=[pl.BlockSpec((TM, c), lambda g: (chunk(g), 0)),
                   pl.BlockSpec((TM, c), lambda g: (chunk(g), 0))],
        out_shape=[jax.ShapeDtypeStruct((rows, c), F32), jax.ShapeDtypeStruct((rows, c), F32)],
        scratch_shapes=[pltpu.VMEM((TM, c), F32), pltpu.VMEM((TM, c), F32), pltpu.VMEM((NB, c), F32)],
        compiler_params=_cparams(("arbitrary",)),
        name="lru_fwd",
    )(xr, xr, xr, conv_w, conv_b, wg, gb, sp)


def _lru_bwd_kernel(alpha, xc_ref, hf_ref, gate_ref, x_ref, g1_ref, wg_ref, gb_ref, sp_ref, wo_ref,
                    lg_ref, lb_ref, o_ref, a_ref, b_ref, y_ref, h_ref):
    g = pl.program_id(0)

    @pl.when(g == 0)
    def _():
        h_ref[...] = jnp.zeros_like(h_ref)

    _rglru_coeffs(xc_ref[...], wg_ref, gb_ref, sp_ref, a_ref, b_ref)
    h = h_ref[...]
    for t in range(TT - 1, -1, -1):
        rows = pl.ds(t * NB, NB)
        h = a_ref[rows, :] * h + b_ref[rows, :]
        y_ref[rows, :] = h
    h_ref[...] = h
    y = ((hf_ref[...] + y_ref[...]) * gate_ref[...]).astype(BF16)
    delta = jnp.dot(y, wo_ref[...], preferred_element_type=F32)
    o_ref[...] = _post_norm(x_ref[...], delta, g1_ref[...], lg_ref[...], lb_ref[...], alpha)


def _lru_bwd(xc, hf, gate, x, g1, wg, gb, sp, w_out, lg, lb, n_lat, n_ctx, alpha):
    rows, c = xc.shape
    d = x.shape[1]
    chunk = lambda g: jnp.where(g < n_ctx, n_lat + (n_ctx - 1 - g), n_lat - 1 - (g - n_ctx))
    tile = lambda g: (chunk(g), 0)
    const = lambda g: (0, 0)
    return pl.pallas_call(
        functools.partial(_lru_bwd_kernel, alpha),
        grid=(n_lat + n_ctx,),
        in_specs=[pl.BlockSpec((TM, c), tile),
                  pl.BlockSpec((TM, c), tile),
                  pl.BlockSpec((TM, c), tile),
                  pl.BlockSpec((TM, d), tile),
                  pl.BlockSpec((None, NB, d), lambda g: (jnp.where(g < n_ctx, 1, 0), 0, 0)),
                  pl.BlockSpec((c, 2 * c), const),
                  pl.BlockSpec((1, 2 * c), const),
                  pl.BlockSpec((1, c), const),
                  pl.BlockSpec((c, d), const),
                  pl.BlockSpec((1, d), const),
                  pl.BlockSpec((1, d), const)],
        out_specs=pl.BlockSpec((TM, d), tile),
        out_shape=jax.ShapeDtypeStruct((rows, d), F32),
        scratch_shapes=[pltpu.VMEM((TM, c), F32), pltpu.VMEM((TM, c), F32), pltpu.VMEM((TM, c), F32),
                        pltpu.VMEM((NB, c), F32)],
        compiler_params=_cparams(("arbitrary",)),
        name="lru_bwd",
    )(xc, hf, gate, x, g1, wg, gb, sp, w_out, lg, lb)


def _softplus_kernel(x_ref, o_ref):
    x = x_ref[...]
    o_ref[...] = jax.nn.softplus(x)


def _gate_weights(gate_w, gate_b):
    nblk, blk, _ = gate_w.shape
    c = nblk * blk
    eye = jnp.eye(nblk, dtype=gate_w.dtype)
    wr = jnp.einsum('nde,nm->ndme', gate_w[..., :blk], eye).reshape(c, c)
    wi = jnp.einsum('nde,nm->ndme', gate_w[..., blk:], eye).reshape(c, c)
    w = jnp.concatenate([wr, wi], axis=1).astype(BF16)
    b = jnp.concatenate([gate_b[:, :blk].reshape(1, c), gate_b[:, blk:].reshape(1, c)], axis=1)
    return w, b


def _qk_prep_kernel(qkv_ref, cos_ref, sa_ref, sb_ref, qg_ref, kg_ref, q_ref, k_ref, v_ref):
    cos, sa, sb = cos_ref[...], sa_ref[...], sb_ref[...]
    nq = N_Q_HEADS * HEAD_DIM
    nk = N_KV_HEADS * HEAD_DIM

    def norm_rope(xh, gain):
        ms = jnp.mean(xh * xh, axis=-1, keepdims=True)
        xn = xh * lax.rsqrt(ms + RMS_EPS) * gain
        return xn * cos + pltpu.roll(xn, 96, axis=1) * sa + pltpu.roll(xn, 32, axis=1) * sb

    for h in range(N_Q_HEADS):
        sl = slice(h * HEAD_DIM, (h + 1) * HEAD_DIM)
        q_ref[:, sl] = norm_rope(qkv_ref[:, sl], qg_ref[...]).astype(BF16)
    for h in range(N_KV_HEADS):
        sl = slice(h * HEAD_DIM, (h + 1) * HEAD_DIM)
        k_ref[:, sl] = norm_rope(qkv_ref[:, nq + h * HEAD_DIM:nq + (h + 1) * HEAD_DIM], kg_ref[...]).astype(BF16)
    v_ref[...] = qkv_ref[:, nq + nk:].astype(BF16)


def _qk_prep(qkv, cos, sa, sb, qg, kg, lt):
    w = qkv.shape[1]
    nq = N_Q_HEADS * HEAD_DIM
    nk = N_KV_HEADS * HEAD_DIM
    qkv2 = qkv.reshape(lt, NB * w)
    tl = TQ
    const = lambda b, i: (0, 0)
    return pl.pallas_call(
        _qk_prep_kernel,
        grid=(NB, lt // tl),
        in_specs=[pl.BlockSpec((tl, w), lambda b, i: (i, b)),
                  pl.BlockSpec((tl, HEAD_DIM), lambda b, i: (i, 0)),
                  pl.BlockSpec((tl, HEAD_DIM), lambda b, i: (i, 0)),
                  pl.BlockSpec((tl, HEAD_DIM), lambda b, i: (i, 0)),
                  pl.BlockSpec((1, HEAD_DIM), const),
                  pl.BlockSpec((1, HEAD_DIM), const)],
        out_specs=[pl.BlockSpec((None, tl, nq), lambda b, i: (b, i, 0)),
                   pl.BlockSpec((None, tl, nk), lambda b, i: (b, i, 0)),
                   pl.BlockSpec((None, tl, nk), lambda b, i: (b, i, 0))],
        out_shape=[jax.ShapeDtypeStruct((NB, lt, nq), BF16),
                   jax.ShapeDtypeStruct((NB, lt, nk), BF16),
                   jax.ShapeDtypeStruct((NB, lt, nk), BF16)],
        compiler_params=_cparams(("arbitrary", "arbitrary")),
        name="qk_prep",
    )(qkv2, cos, sa, sb, qg, kg)


def _attn_kernel(n_lat_tiles, seq, q_ref, k_ref, v_ref, o_ref):
    qi = pl.program_id(2)
    tq = q_ref.shape[0]
    q4 = jnp.concatenate([q_ref[:, j * HEAD_DIM:(j + 1) * HEAD_DIM] for j in range(Q_PER_KV)], axis=0)
    scale = HEAD_DIM ** -0.5

    def attend(k, v):
        s = lax.dot_general(q4, k, (((1,), (1,)), ((), ())), preferred_element_type=F32) * scale
        m = jnp.max(s, axis=-1, keepdims=True)
        p = jnp.exp(s - m)
        l = jnp.sum(p, axis=-1, keepdims=True)
        o = jnp.dot(p.astype(BF16), v, preferred_element_type=F32) / l
        for j in range(Q_PER_KV):
            o_ref[:, j * HEAD_DIM:(j + 1) * HEAD_DIM] = o[j * tq:(j + 1) * tq].astype(BF16)

    @pl.when(qi < n_lat_tiles)
    def _():
        attend(k_ref[...], v_ref[...])

    @pl.when(qi >= n_lat_tiles)
    def _():
        attend(k_ref[seq:, :], v_ref[seq:, :])


def _attention(q, k, v, seq):
    nb, lt, nq = q.shape
    tq = TQ
    gw = Q_PER_KV * HEAD_DIM
    return pl.pallas_call(
        functools.partial(_attn_kernel, seq // tq, seq),
        grid=(nb, N_KV_HEADS, lt // tq),
        in_specs=[pl.BlockSpec((None, tq, gw), lambda b, h, i: (b, i, h)),
                  pl.BlockSpec((None, lt, HEAD_DIM), lambda b, h, i: (b, 0, h)),
                  pl.BlockSpec((None, lt, HEAD_DIM), lambda b, h, i: (b, 0, h))],
        out_specs=pl.BlockSpec((tq, gw), lambda b, h, i: (i, b * N_KV_HEADS + h)),
        out_shape=jax.ShapeDtypeStruct((lt, nb * nq), BF16),
        compiler_params=_cparams(("arbitrary", "arbitrary", "arbitrary")),
        name="attention",
    )(q, k, v)


def _attn_out_kernel(alpha, o_ref, x_ref, g1_ref, w_ref, lg_ref, lb_ref, out_ref):
    delta = jnp.dot(o_ref[...], w_ref[...], preferred_element_type=F32)
    out_ref[...] = _post_norm(x_ref[...], delta, g1_ref[...], lg_ref[...], lb_ref[...], alpha)


def _attn_out(o, x, g1, w_out, lg, lb, n_lat_tiles, alpha):
    rows, d = x.shape
    nq = o.shape[1]
    const = lambda i: (0, 0)
    return pl.pallas_call(
        functools.partial(_attn_out_kernel, alpha),
        grid=(rows // TM,),
        in_specs=[pl.BlockSpec((TM, nq), lambda i: (i, 0)),
                  pl.BlockSpec((TM, d), lambda i: (i, 0)),
                  pl.BlockSpec((None, NB, d), lambda i: (i // n_lat_tiles, 0, 0)),
                  pl.BlockSpec((nq, d), const),
                  pl.BlockSpec((1, d), const),
                  pl.BlockSpec((1, d), const)],
        out_specs=pl.BlockSpec((TM, d), lambda i: (i, 0)),
        out_shape=jax.ShapeDtypeStruct((rows, d), F32),
        compiler_params=_cparams(("arbitrary",)),
        name="attn_out",
    )(o, x, g1, w_out, lg, lb)


def _rope_tables(seq, ctx_len):
    rows = seq // GRID_W
    row, col = jnp.meshgrid(jnp.arange(rows), jnp.arange(GRID_W), indexing='ij')
    pos = jnp.stack([row.reshape(-1), col.reshape(-1)], axis=-1).astype(F32)
    axis_dim = HEAD_DIM // 2
    inv_freq = ROPE_THETA ** (-jnp.arange(0, axis_dim, 2, dtype=F32) / axis_dim)
    ang = pos[:, :, None] * inv_freq
    cos, sin = jnp.cos(ang), jnp.sin(ang)
    zero = jnp.zeros_like(sin)
    cosf = jnp.concatenate([cos, cos], axis=-1).reshape(seq, HEAD_DIM)
    sa = jnp.concatenate([-sin, zero], axis=-1).reshape(seq, HEAD_DIM)
    sb = jnp.concatenate([zero, sin], axis=-1).reshape(seq, HEAD_DIM)
    pad = lambda t, v: jnp.concatenate([t, jnp.full((ctx_len, HEAD_DIM), v, F32)], axis=0)
    return pad(cosf, 1.0), pad(sa, 0.0), pad(sb, 0.0)


def _top16_rows(s, iota, n, val_ref, idx_ref, base):
    for r in range(PEER_TOPK):
        m = jnp.max(s, axis=0, keepdims=True)
        idx = jnp.min(jnp.where(s == m, iota, float(n)), axis=0, keepdims=True)
        s = jnp.where(iota == idx, -jnp.inf, s)
        val_ref[base + r:base + r + 1, :] = m
        idx_ref[base + r:base + r + 1, :] = idx


_CAND_WIDE = [(a, PEER_TOPK // (a + 1)) for a in range(PEER_TOPK // 2)]
_CAND_TAIL = sum(n for _, n in _CAND_WIDE)
_CAND_USED = _CAND_TAIL + PEER_TOPK // 2
N_CAND = -(-_CAND_USED // 8) * 8


def _cand_codes():
    codes = [a * PEER_TOPK + b for a, n in _CAND_WIDE for b in range(n)]
    codes += [a * PEER_TOPK for a in range(PEER_TOPK // 2, PEER_TOPK)]
    codes += [PEER_TOPK * PEER_TOPK + i for i in range(N_CAND - len(codes))]
    return codes


def _peer_route_kernel(x_ref, sc_ref, sh_ref, wq_ref, sk_ref, code_ref, hb_ref, i1_ref, i2_ref, g_ref,
                       v1_ref, j1_ref, v2_ref, j2_ref, ts_ref, tp_ref, cand_ref, o1_ref, o2_ref, og_ref):
    h = _modulate(x_ref[...], sc_ref[...], sh_ref[...]).astype(BF16)
    hb_ref[...] = h
    hq = jnp.dot(h, wq_ref[...], preferred_element_type=F32).astype(BF16)
    tm = h.shape[0]
    key_iota = lax.broadcasted_iota(I32, (N_KEYS, tm), 0).astype(F32)
    nt = (((1,), (1,)), ((), ()))
    half = PEER_TOPK // 2
    cand_ref[_CAND_USED:, :] = jnp.full((N_CAND - _CAND_USED, tm), -jnp.inf, F32)
    for hd in range(PEER_HEADS):
        for p, (v_ref, j_ref) in enumerate(((v1_ref, j1_ref), (v2_ref, j2_ref))):
            qs = hq[:, (hd * 2 + p) * LANES:(hd * 2 + p + 1) * LANES]
            s = lax.dot_general(sk_ref[p], qs, nt, preferred_element_type=F32)
            _top16_rows(s, key_iota, N_KEYS, v_ref, j_ref, 0)
        off = 0
        for a, n in _CAND_WIDE:
            cand_ref[off:off + n, :] = v1_ref[a:a + 1, :] + v2_ref[0:n, :]
            off += n
        cand_ref[_CAND_TAIL:_CAND_USED, :] = v1_ref[half:, :] + v2_ref[0:1, :]
        _top16_rows(cand_ref[...], code_ref[...], 2 * PEER_TOPK * PEER_TOPK, ts_ref, tp_ref, 0)
        ts = ts_ref[...]
        pos = tp_ref[...]
        ak = jnp.floor(pos * (1.0 / PEER_TOPK))
        bk = pos - PEER_TOPK * ak
        i1 = jnp.zeros_like(ts)
        i2 = jnp.zeros_like(ts)
        for a in range(PEER_TOPK):
            i1 = jnp.where(ak == a, j1_ref[a:a + 1, :], i1)
            i2 = jnp.where(bk == a, j2_ref[a:a + 1, :], i2)
        e = jnp.exp(ts - ts[0:1, :])
        rows = slice(hd * PEER_TOPK, (hd + 1) * PEER_TOPK)
        o1_ref[rows, :] = i1
        o2_ref[rows, :] = i2
        og_ref[rows, :] = e / jnp.sum(e, axis=0, keepdims=True)
    i1_ref[...] = o1_ref[...].T.astype(I32)
    i2_ref[...] = o2_ref[...].T.astype(I32)
    g_ref[...] = og_ref[...].T


def _peer_route(x, sc, sh, w_q, sub_keys, n_lat_tiles):
    rows, d = x.shape
    nq = w_q.shape[1]
    sel = lambda i: (i // n_lat_tiles, 0, 0)
    tile = lambda i: (i, 0)
    codes = jnp.broadcast_to(jnp.asarray(_cand_codes(), F32)[:, None], (N_CAND, TM))
    return pl.pallas_call(
        _peer_route_kernel,
        grid=(rows // TM,),
        in_specs=[pl.BlockSpec((TM, d), tile),
                  pl.BlockSpec((None, NB, d), sel),
                  pl.BlockSpec((None, NB, d), sel),
                  pl.BlockSpec((d, nq), lambda i: (0, 0)),
                  pl.BlockSpec(sub_keys.shape, lambda i: (0, 0, 0)),
                  pl.BlockSpec((N_CAND, TM), lambda i: (0, 0))],
        out_specs=[pl.BlockSpec((TM, d), tile),
                   pl.BlockSpec((TM, N_SLOTS), tile),
                   pl.BlockSpec((TM, N_SLOTS), tile),
                   pl.BlockSpec((TM, N_SLOTS), tile)],
        out_shape=[jax.ShapeDtypeStruct((rows, d), BF16),
                   jax.ShapeDtypeStruct((rows, N_SLOTS), I32),
                   jax.ShapeDtypeStruct((rows, N_SLOTS), I32),
                   jax.ShapeDtypeStruct((rows, N_SLOTS), F32)],
        scratch_shapes=[pltpu.VMEM((PEER_TOPK, TM), F32)] * 6
                       + [pltpu.VMEM((N_CAND, TM), F32)]
                       + [pltpu.VMEM((N_SLOTS, TM), F32)] * 3,
        compiler_params=_cparams(("arbitrary",)),
        name="peer_route",
    )(x, sc, sh, w_q, sub_keys, codes)


def _peer_act_kernel(hb_ref, u_ref, i1_ref, i2_ref, g_ref, w_ref, z_ref):
    cj = pl.program_id(1)

    @pl.when(cj == 0)
    def _():
        z_ref[...] = jnp.zeros_like(z_ref)

    s = lax.dot_general(hb_ref[...], u_ref[...], (((1,), (1,)), ((), ())), preferred_element_type=F32)
    i1 = i1_ref[...]
    i2 = i2_ref[...]
    z = z_ref[...]
    for q in range(EXP_CHUNK // N_KEYS):
        picked = jnp.take_along_axis(s[:, q * N_KEYS:(q + 1) * N_KEYS], i2, axis=1)
        z = jnp.where(i1 == cj * (EXP_CHUNK // N_KEYS) + q, picked, z)
    z_ref[...] = z

    @pl.when(cj == pl.num_programs(1) - 1)
    def _():
        w_ref[...] = g_ref[...] * _gelu(z)


def _peer_act(hb, u_tab, i1, i2, g):
    rows, d = hb.shape
    n_exp = u_tab.shape[0]
    tile = lambda i, j: (i, 0)
    return pl.pallas_call(
        _peer_act_kernel,
        grid=(rows // TM_ACT, n_exp // EXP_CHUNK),
        in_specs=[pl.BlockSpec((TM_ACT, d), tile),
                  pl.BlockSpec((EXP_CHUNK, d), lambda i, j: (j, 0)),
                  pl.BlockSpec((TM_ACT, N_SLOTS), tile),
                  pl.BlockSpec((TM_ACT, N_SLOTS), tile),
                  pl.BlockSpec((TM_ACT, N_SLOTS), tile)],
        out_specs=pl.BlockSpec((TM_ACT, N_SLOTS), tile),
        out_shape=jax.ShapeDtypeStruct((rows, N_SLOTS), F32),
        scratch_shapes=[pltpu.VMEM((TM_ACT, N_SLOTS), F32)],
        compiler_params=_cparams(("arbitrary", "arbitrary")),
        name="peer_act",
    )(hb, u_tab, i1, i2, g)


def _peer_out_kernel(alpha, i1_ref, i2_ref, w_ref, v_ref, x_ref, g2_ref, lg_ref, lb_ref, o_ref,
                     wbuf_ref, acc_ref):
    vj = pl.program_id(1)
    tm = x_ref.shape[0]
    groups = VAL_CHUNK // N_KEYS

    @pl.when(vj == 0)
    def _():
        acc_ref[...] = jnp.zeros_like(acc_ref)
        key_iota = lax.broadcasted_iota(I32, (N_KEYS, N_SLOTS), 0)
        nt = (((1,), (1,)), ((), ()))

        def body(tb, carry):
            for u in range(ASM_UNROLL):
                t = tb * ASM_UNROLL + u
                row = pl.ds(t, 1)
                i1 = jnp.broadcast_to(i1_ref[row, :], (N_KEYS, N_SLOTS))
                i2 = jnp.broadcast_to(i2_ref[row, :], (N_KEYS, N_SLOTS))
                w = jnp.broadcast_to(w_ref[row, :], (N_KEYS, N_SLOTS))
                lhs = jnp.where(i1 == key_iota, w, 0.0).astype(BF16)
                rhs = jnp.where(i2 == key_iota, 1.0, 0.0).astype(BF16)
                wbuf_ref[pl.ds(pl.multiple_of(t * W_STRIDE, 8), N_KEYS), :] = lax.dot_general(
                    lhs, rhs, nt, preferred_element_type=F32)
            return carry

        lax.fori_loop(0, tm // ASM_UNROLL, body, 0)

    lhs = jnp.concatenate(
        [wbuf_ref[pl.ds(vj * groups + q, tm, stride=W_STRIDE), :] for q in range(groups)], axis=1).astype(BF16)
    acc_ref[...] += jnp.dot(lhs, v_ref[...], preferred_element_type=F32)

    @pl.when(vj == pl.num_programs(1) - 1)
    def _():
        o_ref[...] = _post_norm(x_ref[...], acc_ref[...], g2_ref[...], lg_ref[...], lb_ref[...], alpha)


def _peer_out(i1, i2, w, v_tab, x, g2, lg, lb, n_lat_tiles, alpha):
    rows, d = x.shape
    n_exp = v_tab.shape[0]
    tile = lambda i, j: (i, 0)
    const = lambda i, j: (0, 0)
    return pl.pallas_call(
        functools.partial(_peer_out_kernel, alpha),
        grid=(rows // TM, n_exp // VAL_CHUNK),
        in_specs=[pl.BlockSpec((TM, N_SLOTS), tile),
                  pl.BlockSpec((TM, N_SLOTS), tile),
                  pl.BlockSpec((TM, N_SLOTS), tile),
                  pl.BlockSpec((VAL_CHUNK, d), lambda i, j: (j, 0)),
                  pl.BlockSpec((TM, d), tile),
                  pl.BlockSpec((None, NB, d), lambda i, j: (i // n_lat_tiles, 0, 0)),
                  pl.BlockSpec((1, d), const),
                  pl.BlockSpec((1, d), const)],
        out_specs=pl.BlockSpec((TM, d), tile),
        out_shape=jax.ShapeDtypeStruct((rows, d), F32),
        scratch_shapes=[pltpu.VMEM((TM * W_STRIDE, N_KEYS), F32), pltpu.VMEM((TM, d), F32)],
        compiler_params=_cparams(("arbitrary", "arbitrary")),
        name="peer_out",
    )(i1, i2, w, v_tab, x, g2, lg, lb)


def kernel(x, c, ctx, c_ctx, mod_w, mod_b, ln_g, ln_b, lru_w_in, lru_conv_w, lru_conv_b, lru_gate_w, lru_gate_b,
           lru_lambda, lru_w_out, attn_w_qkv, attn_q_gain, attn_k_gain, attn_w_out, peer_w_q, peer_sub_keys,
           peer_u, peer_v):
    nb, seq, d = x.shape
    ctx_len = ctx.shape[1]
    depth = mod_w.shape[0]
    assert nb == NB and seq % GRID_W == 0 and seq % TQ == 0 and ctx_len % TQ == 0
    for t in (TM, TM_ACT):
        assert (seq * NB) % t == 0 and (ctx_len * NB) % t == 0
    lt = seq + ctx_len
    n_lat, n_ctx = seq * NB // TM, ctx_len * NB // TM
    alpha = (2 * depth) ** 0.25

    xs = jnp.concatenate([x.transpose(1, 0, 2), ctx.transpose(1, 0, 2)], axis=0).reshape(lt * NB, d)

    cond = jnp.concatenate([c, c_ctx[None], jnp.zeros((7, d), F32)], axis=0)
    mtab = _modulation_table(cond, mod_w, mod_b)
    lat = mtab[:, :NB].reshape(depth, NB, 6, d).transpose(0, 2, 1, 3)
    cx = jnp.broadcast_to(mtab[:, NB].reshape(depth, 6, 1, d), (depth, 6, NB, d))
    mods = jnp.stack([lat, cx], axis=2)

    cos, sa, sb = _rope_tables(seq, ctx_len)
    lam = lru_lambda.reshape(-1, lru_lambda.shape[-1])
    softplus_neg_lam = pl.pallas_call(
        _softplus_kernel, out_shape=jax.ShapeDtypeStruct(lam.shape, F32), name="softplus")(-lam)
    softplus_neg_lam = softplus_neg_lam.reshape(lru_lambda.shape)

    for i in range(depth):
        sh1, sc1, g1, sh2, sc2, g2 = (mods[i, j] for j in range(6))
        lg1, lb1 = ln_g[i, 0][None], ln_b[i, 0][None]
        lg2, lb2 = ln_g[i, 1][None], ln_b[i, 1][None]
        j = i // 2
        if i % 2 == 0:
            gate, xr = _lru_in(xs, sc1, sh1, lru_w_in[j].astype(BF16), n_lat)
            wg0, gb0 = _gate_weights(lru_gate_w[j, 0], lru_gate_b[j, 0])
            wg1, gb1 = _gate_weights(lru_gate_w[j, 1], lru_gate_b[j, 1])
            xc, hf = _lru_fwd(xr, lru_conv_w[j], lru_conv_b[j][None], wg0, gb0, softplus_neg_lam[j, 0][None],
                              n_lat, n_ctx)
            xs = _lru_bwd(xc, hf, gate, xs, g1, wg1, gb1, softplus_neg_lam[j, 1][None],
                          lru_w_out[j].astype(BF16), lg1, lb1, n_lat, n_ctx, alpha)
        else:
            qkv = _modmm(xs, sc1, sh1, attn_w_qkv[j].astype(BF16), n_lat, "qkv")
            q, k, v = _qk_prep(qkv, cos, sa, sb, attn_q_gain[j][None], attn_k_gain[j][None], lt)
            o = _attention(q, k, v, seq).reshape(lt * NB, N_Q_HEADS * HEAD_DIM)
            xs = _attn_out(o, xs, g1, attn_w_out[j].astype(BF16), lg1, lb1, n_lat, alpha)
        hb, i1, i2, g = _peer_route(xs, sc2, sh2, peer_w_q[i].astype(BF16), peer_sub_keys[i].astype(BF16), n_lat)
        w = _peer_act(hb, peer_u[i].astype(BF16), i1, i2, g)
        xs = _peer_out(i1, i2, w, peer_v[i].astype(BF16), xs, g2, lg2, lb2, n_lat, alpha)

    return xs.reshape(lt, NB, d)[:seq].transpose(1, 0, 2)
```

```python
import functools

import jax
import jax.numpy as jnp
from jax import lax
from jax.experimental import pallas as pl
from jax.experimental.pallas import tpu as pltpu

F32 = jnp.float32
BF16 = jnp.bfloat16
I32 = jnp.int32

NB = 16
GRID_W = 64
CONV_W = 4
LRU_C = 8.0
N_Q_HEADS = 8
N_KV_HEADS = 2
HEAD_DIM = 128
Q_PER_KV = N_Q_HEADS // N_KV_HEADS
ROPE_THETA = 10000.0
PEER_HEADS = 8
N_KEYS = 128
PEER_TOPK = 16
N_SLOTS = PEER_HEADS * PEER_TOPK
LN_EPS = 1e-6
RMS_EPS = 1e-6
LANES = 128
VMEM_LIMIT = 56 * 1024 * 1024

TM = 256
TT = TM // NB
TQ = 256
TM_ACT = 1024
EXP_CHUNK = 1024
VAL_CHUNK = 1024
W_STRIDE = N_KEYS + 8
ASM_UNROLL = 16
TM_OUT = 512
ASM_TM = 128


def _cparams(sem):
    return pltpu.CompilerParams(dimension_semantics=sem, vmem_limit_bytes=VMEM_LIMIT)


def _gelu(x):
    return 0.5 * x * (1.0 + lax.erf(x * 0.7071067811865476))


def _modulate(x, sc, sh):
    rows, d = x.shape
    x3 = x.reshape(rows // NB, NB, d)
    return (x3 * (1.0 + sc)[None] + sh[None]).reshape(rows, d)


def _post_norm(x, delta, gate, lg, lb, alpha):
    rows, d = x.shape
    v = alpha * x + (delta.reshape(rows // NB, NB, d) * gate[None]).reshape(rows, d)
    mu = jnp.mean(v, axis=-1, keepdims=True)
    vc = v - mu
    var = jnp.mean(vc * vc, axis=-1, keepdims=True)
    return vc * lax.rsqrt(var + LN_EPS) * lg + lb


def _mod_kernel(c_ref, w_ref, b_ref, o_ref):
    c = c_ref[...]
    s = (c * jax.nn.sigmoid(c)).astype(BF16)
    o_ref[...] = jnp.dot(s, w_ref[...].astype(BF16), preferred_element_type=F32) + b_ref[...]


def _modulation_table(cond, mod_w, mod_b):
    depth, d, d6 = mod_w.shape
    nblk = d6 // d
    return pl.pallas_call(
        _mod_kernel,
        grid=(depth, nblk),
        in_specs=[pl.BlockSpec((cond.shape[0], d), lambda l, j: (0, 0)),
                  pl.BlockSpec((None, d, d), lambda l, j: (l, 0, j)),
                  pl.BlockSpec((None, 1, d), lambda l, j: (l, 0, j))],
        out_specs=pl.BlockSpec((None, cond.shape[0], d), lambda l, j: (l, 0, j)),
        out_shape=jax.ShapeDtypeStruct((depth, cond.shape[0], d6), F32),
        compiler_params=_cparams(("arbitrary", "arbitrary")),
        name="modulation",
    )(cond, mod_w, mod_b.reshape(depth, 1, d6))


def _lru_in_kernel(x_ref, sc_ref, sh_ref, w_ref, gate_ref, xr_ref):
    h = _modulate(x_ref[...], sc_ref[...], sh_ref[...]).astype(BF16)
    u = jnp.dot(h, w_ref[...], preferred_element_type=F32)
    c = gate_ref.shape[-1]
    gate_ref[...] = _gelu(u[:, :c])
    xr_ref[...] = u[:, c:]


def _lru_in(x, sc, sh, w_in, n_lat_tiles):
    rows, d = x.shape
    c2 = w_in.shape[1]
    c = c2 // 2
    sel = lambda i: (i // n_lat_tiles, 0, 0)
    return pl.pallas_call(
        _lru_in_kernel,
        grid=(rows // TM,),
        in_specs=[pl.BlockSpec((TM, d), lambda i: (i, 0)),
                  pl.BlockSpec((None, NB, d), sel),
                  pl.BlockSpec((None, NB, d), sel),
                  pl.BlockSpec((d, c2), lambda i: (0, 0))],
        out_specs=[pl.BlockSpec((TM, c), lambda i: (i, 0)),
                   pl.BlockSpec((TM, c), lambda i: (i, 0))],
        out_shape=[jax.ShapeDtypeStruct((rows, c), F32), jax.ShapeDtypeStruct((rows, c), F32)],
        compiler_params=_cparams(("arbitrary",)),
        name="lru_in",
    )(x, sc, sh, w_in)


def _modmm_kernel(x_ref, sc_ref, sh_ref, w_ref, o_ref):
    h = _modulate(x_ref[...], sc_ref[...], sh_ref[...]).astype(BF16)
    o_ref[...] = jnp.dot(h, w_ref[...], preferred_element_type=F32)


def _modmm(x, sc, sh, w, n_lat_tiles, name):
    rows, d = x.shape
    n = w.shape[1]
    sel = lambda i: (i // n_lat_tiles, 0, 0)
    return pl.pallas_call(
        _modmm_kernel,
        grid=(rows // TM,),
        in_specs=[pl.BlockSpec((TM, d), lambda i: (i, 0)),
                  pl.BlockSpec((None, NB, d), sel),
                  pl.BlockSpec((None, NB, d), sel),
                  pl.BlockSpec((d, n), lambda i: (0, 0))],
        out_specs=pl.BlockSpec((TM, n), lambda i: (i, 0)),
        out_shape=jax.ShapeDtypeStruct((rows, n), F32),
        compiler_params=_cparams(("arbitrary",)),
        name=name,
    )(x, sc, sh, w)


def _rglru_coeffs(xc, wg_ref, gb_ref, sp_ref, a_ref, b_ref):
    c = xc.shape[-1]
    gates = jnp.dot(xc.astype(BF16), wg_ref[...], preferred_element_type=F32) + gb_ref[...]
    r = jax.nn.sigmoid(gates[:, :c])
    i = jax.nn.sigmoid(gates[:, c:])
    log_a = -LRU_C * r * sp_ref[...]
    a = jnp.exp(log_a)
    a_ref[...] = a
    b_ref[...] = jnp.sqrt(-(a * a + 1.0) * jnp.tanh(log_a)) * i * xc


def _lru_fwd_kernel(n_lat, n_ctx, xr_ref, prev_ref, next_ref, cw_ref, cb_ref, wg_ref, gb_ref, sp_ref,
                    xc_ref, hf_ref, a_ref, b_ref, h_ref):
    g = pl.program_id(0)

    @pl.when(g == 0)
    def _():
        h_ref[...] = jnp.zeros_like(h_ref)

    first = jnp.logical_or(g == 0, g == n_ctx)
    last = jnp.logical_or(g == n_ctx - 1, g == n_ctx + n_lat - 1)
    prev = jnp.where(first, 0.0, prev_ref[...])
    nxt = jnp.where(last, 0.0, next_ref[...])
    ext = jnp.concatenate([prev, xr_ref[...], nxt], axis=0)
    cw = cw_ref[...]
    xc = cb_ref[...] + cw[0:1] * ext[0:TM]
    for j in range(1, CONV_W):
        xc = xc + cw[j:j + 1] * ext[j * NB:j * NB + TM]
    xc_ref[...] = xc
    _rglru_coeffs(xc, wg_ref, gb_ref, sp_ref, a_ref, b_ref)
    h = h_ref[...]
    for t in range(TT):
        rows = pl.ds(t * NB, NB)
        h = a_ref[rows, :] * h + b_ref[rows, :]
        hf_ref[rows, :] = h
    h_ref[...] = h


def _lru_fwd(xr, conv_w, conv_b, wg, gb, sp, n_lat, n_ctx):
    rows, c = xr.shape
    halo_p, halo_n = 2 * NB, NB
    per_p, per_n = TM // halo_p, TM // halo_n
    chunk = lambda g: jnp.where(g < n_ctx, n_lat + g, g - n_ctx)
    const = lambda g: (0, 0)
    return pl.pallas_call(
        functools.partial(_lru_fwd_kernel, n_lat, n_ctx),
        grid=(n_lat + n_ctx,),
        in_specs=[pl.BlockSpec((TM, c), lambda g: (chunk(g), 0)),
                  pl.BlockSpec((halo_p, c), lambda g: (jnp.maximum(chunk(g) * per_p - 1, 0), 0)),
                  pl.BlockSpec((halo_n, c), lambda g: (jnp.minimum((chunk(g) + 1) * per_n, rows // halo_n - 1), 0)),
                  pl.BlockSpec((CONV_W, c), const),
                  pl.BlockSpec((1, c), const),
                  pl.BlockSpec((c, 2 * c), const),
                  pl.BlockSpec((1, 2 * c), const),
                  pl.BlockSpec((1, c), const)],
        out_specs=[pl.BlockSpec((TM, c), lambda g: (chunk(g), 0)),
                   pl.BlockSpec((TM, c), lambda g: (chunk(g), 0))],
        out_shape=[jax.ShapeDtypeStruct((rows, c), F32), jax.ShapeDtypeStruct((rows, c), F32)],
        scratch_shapes=[pltpu.VMEM((TM, c), F32), pltpu.VMEM((TM, c), F32), pltpu.VMEM((NB, c), F32)],
        compiler_params=_cparams(("arbitrary",)),
        name="lru_fwd",
    )(xr, xr, xr, conv_w, conv_b, wg, gb, sp)


def _lru_bwd_kernel(alpha, xc_ref, hf_ref, gate_ref, x_ref, g1_ref, wg_ref, gb_ref, sp_ref, wo_ref,
                    lg_ref, lb_ref, o_ref, a_ref, b_ref, y_ref, h_ref):
    g = pl.program_id(0)

    @pl.when(g == 0)
    def _():
        h_ref[...] = jnp.zeros_like(h_ref)

    _rglru_coeffs(xc_ref[...], wg_ref, gb_ref, sp_ref, a_ref, b_ref)
    h = h_ref[...]
    for t in range(TT - 1, -1, -1):
        rows = pl.ds(t * NB, NB)
        h = a_ref[rows, :] * h + b_ref[rows, :]
        y_ref[rows, :] = h
    h_ref[...] = h
    y = ((hf_ref[...] + y_ref[...]) * gate_ref[...]).astype(BF16)
    delta = jnp.dot(y, wo_ref[...], preferred_element_type=F32)
    o_ref[...] = _post_norm(x_ref[...], delta, g1_ref[...], lg_ref[...], lb_ref[...], alpha)


def _lru_bwd(xc, hf, gate, x, g1, wg, gb, sp, w_out, lg, lb, n_lat, n_ctx, alpha):
    rows, c = xc.shape
    d = x.shape[1]
    chunk = lambda g: jnp.where(g < n_ctx, n_lat + (n_ctx - 1 - g), n_lat - 1 - (g - n_ctx))
    tile = lambda g: (chunk(g), 0)
    const = lambda g: (0, 0)
    return pl.pallas_call(
        functools.partial(_lru_bwd_kernel, alpha),
        grid=(n_lat + n_ctx,),
        in_specs=[pl.BlockSpec((TM, c), tile),
                  pl.BlockSpec((TM, c), tile),
                  pl.BlockSpec((TM, c), tile),
                  pl.BlockSpec((TM, d), tile),
                  pl.BlockSpec((None, NB, d), lambda g: (jnp.where(g < n_ctx, 1, 0), 0, 0)),
                  pl.BlockSpec((c, 2 * c), const),
                  pl.BlockSpec((1, 2 * c), const),
                  pl.BlockSpec((1, c), const),
                  pl.BlockSpec((c, d), const),
                  pl.BlockSpec((1, d), const),
                  pl.BlockSpec((1, d), const)],
        out_specs=pl.BlockSpec((TM, d), tile),
        out_shape=jax.ShapeDtypeStruct((rows, d), F32),
        scratch_shapes=[pltpu.VMEM((TM, c), F32), pltpu.VMEM((TM, c), F32), pltpu.VMEM((TM, c), F32),
                        pltpu.VMEM((NB, c), F32)],
        compiler_params=_cparams(("arbitrary",)),
        name="lru_bwd",
    )(xc, hf, gate, x, g1, wg, gb, sp, w_out, lg, lb)


def _softplus_kernel(x_ref, o_ref):
    x = x_ref[...]
    o_ref[...] = jax.nn.softplus(x)


def _gate_weights(gate_w, gate_b):
    nblk, blk, _ = gate_w.shape
    c = nblk * blk
    eye = jnp.eye(nblk, dtype=gate_w.dtype)
    wr = jnp.einsum('nde,nm->ndme', gate_w[..., :blk], eye).reshape(c, c)
    wi = jnp.einsum('nde,nm->ndme', gate_w[..., blk:], eye).reshape(c, c)
    w = jnp.concatenate([wr, wi], axis=1).astype(BF16)
    b = jnp.concatenate([gate_b[:, :blk].reshape(1, c), gate_b[:, blk:].reshape(1, c)], axis=1)
    return w, b


def _qk_prep_kernel(qkv_ref, cos_ref, sa_ref, sb_ref, qg_ref, kg_ref, q_ref, k_ref, v_ref):
    cos, sa, sb = cos_ref[...], sa_ref[...], sb_ref[...]
    nq = N_Q_HEADS * HEAD_DIM
    nk = N_KV_HEADS * HEAD_DIM

    def norm_rope(xh, gain):
        ms = jnp.mean(xh * xh, axis=-1, keepdims=True)
        xn = xh * lax.rsqrt(ms + RMS_EPS) * gain
        return xn * cos + pltpu.roll(xn, 96, axis=1) * sa + pltpu.roll(xn, 32, axis=1) * sb

    for h in range(N_Q_HEADS):
        sl = slice(h * HEAD_DIM, (h + 1) * HEAD_DIM)
        q_ref[:, sl] = norm_rope(qkv_ref[:, sl], qg_ref[...]).astype(BF16)
    for h in range(N_KV_HEADS):
        sl = slice(h * HEAD_DIM, (h + 1) * HEAD_DIM)
        k_ref[:, sl] = norm_rope(qkv_ref[:, nq + h * HEAD_DIM:nq + (h + 1) * HEAD_DIM], kg_ref[...]).astype(BF16)
    v_ref[...] = qkv_ref[:, nq + nk:].astype(BF16)


def _qk_prep(qkv, cos, sa, sb, qg, kg, lt):
    w = qkv.shape[1]
    nq = N_Q_HEADS * HEAD_DIM
    nk = N_KV_HEADS * HEAD_DIM
    qkv2 = qkv.reshape(lt, NB * w)
    tl = TQ
    const = lambda b, i: (0, 0)
    return pl.pallas_call(
        _qk_prep_kernel,
        grid=(NB, lt // tl),
        in_specs=[pl.BlockSpec((tl, w), lambda b, i: (i, b)),
                  pl.BlockSpec((tl, HEAD_DIM), lambda b, i: (i, 0)),
                  pl.BlockSpec((tl, HEAD_DIM), lambda b, i: (i, 0)),
                  pl.BlockSpec((tl, HEAD_DIM), lambda b, i: (i, 0)),
                  pl.BlockSpec((1, HEAD_DIM), const),
                  pl.BlockSpec((1, HEAD_DIM), const)],
        out_specs=[pl.BlockSpec((None, tl, nq), lambda b, i: (b, i, 0)),
                   pl.BlockSpec((None, tl, nk), lambda b, i: (b, i, 0)),
                   pl.BlockSpec((None, tl, nk), lambda b, i: (b, i, 0))],
        out_shape=[jax.ShapeDtypeStruct((NB, lt, nq), BF16),
                   jax.ShapeDtypeStruct((NB, lt, nk), BF16),
                   jax.ShapeDtypeStruct((NB, lt, nk), BF16)],
        compiler_params=_cparams(("arbitrary", "arbitrary")),
        name="qk_prep",
    )(qkv2, cos, sa, sb, qg, kg)


def _attn_kernel(n_lat_tiles, seq, q_ref, k_ref, v_ref, o_ref):
    qi = pl.program_id(2)
    tq = q_ref.shape[0]
    scale = HEAD_DIM ** -0.5

    def attend(k, v):
        for j in range(Q_PER_KV):
            cols = slice(j * HEAD_DIM, (j + 1) * HEAD_DIM)
            s = lax.dot_general(q_ref[:, cols], k, (((1,), (1,)), ((), ())), preferred_element_type=F32) * scale
            m = jnp.max(s, axis=-1, keepdims=True)
            p = jnp.exp(s - m)
            l = jnp.sum(p, axis=-1, keepdims=True)
            o = jnp.dot(p.astype(BF16), v, preferred_element_type=F32) / l
            o_ref[:, cols] = o.astype(BF16)

    @pl.when(qi < n_lat_tiles)
    def _():
        attend(k_ref[...], v_ref[...])

    @pl.when(qi >= n_lat_tiles)
    def _():
        attend(k_ref[seq:, :], v_ref[seq:, :])


def _attention(q, k, v, seq):
    nb, lt, nq = q.shape
    tq = TQ
    gw = Q_PER_KV * HEAD_DIM
    return pl.pallas_call(
        functools.partial(_attn_kernel, seq // tq, seq),
        grid=(nb, N_KV_HEADS, lt // tq),
        in_specs=[pl.BlockSpec((None, tq, gw), lambda b, h, i: (b, i, h)),
                  pl.BlockSpec((None, lt, HEAD_DIM), lambda b, h, i: (b, 0, h)),
                  pl.BlockSpec((None, lt, HEAD_DIM), lambda b, h, i: (b, 0, h))],
        out_specs=pl.BlockSpec((tq, gw), lambda b, h, i: (i, b * N_KV_HEADS + h)),
        out_shape=jax.ShapeDtypeStruct((lt, nb * nq), BF16),
        compiler_params=_cparams(("arbitrary", "arbitrary", "arbitrary")),
        name="attention",
    )(q, k, v)


def _attn_out_kernel(alpha, o_ref, x_ref, g1_ref, w_ref, lg_ref, lb_ref, out_ref):
    delta = jnp.dot(o_ref[...], w_ref[...], preferred_element_type=F32)
    out_ref[...] = _post_norm(x_ref[...], delta, g1_ref[...], lg_ref[...], lb_ref[...], alpha)


def _attn_out(o, x, g1, w_out, lg, lb, n_lat_tiles, alpha):
    rows, d = x.shape
    nq = o.shape[1]
    const = lambda i: (0, 0)
    return pl.pallas_call(
        functools.partial(_attn_out_kernel, alpha),
        grid=(rows // TM,),
        in_specs=[pl.BlockSpec((TM, nq), lambda i: (i, 0)),
                  pl.BlockSpec((TM, d), lambda i: (i, 0)),
                  pl.BlockSpec((None, NB, d), lambda i: (i // n_lat_tiles, 0, 0)),
                  pl.BlockSpec((nq, d), const),
                  pl.BlockSpec((1, d), const),
                  pl.BlockSpec((1, d), const)],
        out_specs=pl.BlockSpec((TM, d), lambda i: (i, 0)),
        out_shape=jax.ShapeDtypeStruct((rows, d), F32),
        compiler_params=_cparams(("arbitrary",)),
        name="attn_out",
    )(o, x, g1, w_out, lg, lb)


def _rope_tables(seq, ctx_len):
    rows = seq // GRID_W
    row, col = jnp.meshgrid(jnp.arange(rows), jnp.arange(GRID_W), indexing='ij')
    pos = jnp.stack([row.reshape(-1), col.reshape(-1)], axis=-1).astype(F32)
    axis_dim = HEAD_DIM // 2
    inv_freq = ROPE_THETA ** (-jnp.arange(0, axis_dim, 2, dtype=F32) / axis_dim)
    ang = pos[:, :, None] * inv_freq
    cos, sin = jnp.cos(ang), jnp.sin(ang)
    zero = jnp.zeros_like(sin)
    cosf = jnp.concatenate([cos, cos], axis=-1).reshape(seq, HEAD_DIM)
    sa = jnp.concatenate([-sin, zero], axis=-1).reshape(seq, HEAD_DIM)
    sb = jnp.concatenate([zero, sin], axis=-1).reshape(seq, HEAD_DIM)
    pad = lambda t, v: jnp.concatenate([t, jnp.full((ctx_len, HEAD_DIM), v, F32)], axis=0)
    return pad(cosf, 1.0), pad(sa, 0.0), pad(sb, 0.0)


def _top16_rows(s, code, n, val_ref, idx_ref, cols):
    for r in range(PEER_TOPK):
        m = jnp.max(s, axis=0, keepdims=True)
        idx = jnp.min(jnp.where(s == m, code, float(n)), axis=0, keepdims=True)
        s = jnp.where(code == idx, -jnp.inf, s)
        val_ref[r:r + 1, cols] = m
        idx_ref[r:r + 1, cols] = idx


def _top16_keys(s, val_ref, idx_ref, cols):
    half = N_KEYS // 2
    a, b = s[:half], s[half:]
    row = lax.broadcasted_iota(I32, a.shape, 0).astype(F32)
    swap = b > a
    hi, lo = jnp.maximum(a, b), jnp.minimum(a, b)
    hi_idx = jnp.where(swap, row + half, row)
    lo_idx = jnp.where(swap, row, row + half)
    for r in range(PEER_TOPK):
        m = jnp.max(hi, axis=0, keepdims=True)
        idx = jnp.min(jnp.where(hi == m, hi_idx, float(N_KEYS)), axis=0, keepdims=True)
        taken = hi_idx == idx
        hi = jnp.where(taken, lo, hi)
        hi_idx = jnp.where(taken, lo_idx, hi_idx)
        lo = jnp.where(taken, -jnp.inf, lo)
        val_ref[r:r + 1, cols] = m
        idx_ref[r:r + 1, cols] = idx


_CAND_WIDE = [(a, PEER_TOPK // (a + 1)) for a in range(PEER_TOPK // 2)]
_CAND_TAIL = sum(n for _, n in _CAND_WIDE)
_CAND_USED = _CAND_TAIL + PEER_TOPK // 2
N_CAND = -(-_CAND_USED // 8) * 8


def _cand_codes():
    codes = [a * PEER_TOPK + b for a, n in _CAND_WIDE for b in range(n)]
    codes += [a * PEER_TOPK for a in range(PEER_TOPK // 2, PEER_TOPK)]
    codes += [PEER_TOPK * PEER_TOPK + i for i in range(N_CAND - len(codes))]
    return codes


def _peer_route_kernel(x_ref, sc_ref, sh_ref, wq_ref, sk_ref, code_ref, hb_ref, i1_ref, i2_ref, g_ref,
                       v1_ref, j1_ref, v2_ref, j2_ref, ts_ref, tp_ref, cand_ref, o1_ref, o2_ref, og_ref):
    h = _modulate(x_ref[...], sc_ref[...], sh_ref[...]).astype(BF16)
    hb_ref[...] = h
    hq = jnp.dot(h, wq_ref[...], preferred_element_type=F32).astype(BF16)
    tm = h.shape[0]
    nt = (((1,), (1,)), ((), ()))
    half = PEER_TOPK // 2
    cand_ref[_CAND_USED:, :] = jnp.full((N_CAND - _CAND_USED, tm), -jnp.inf, F32)
    for hd in range(PEER_HEADS):
        for p, (v_ref, j_ref) in enumerate(((v1_ref, j1_ref), (v2_ref, j2_ref))):
            qs = hq[:, (hd * 2 + p) * LANES:(hd * 2 + p + 1) * LANES]
            s = lax.dot_general(sk_ref[p], qs, nt, preferred_element_type=F32)
            for lb in range(tm // LANES):
                cols = slice(lb * LANES, (lb + 1) * LANES)
                _top16_keys(s[:, cols], v_ref, j_ref, cols)
        off = 0
        for a, n in _CAND_WIDE:
            cand_ref[off:off + n, :] = v1_ref[a:a + 1, :] + v2_ref[0:n, :]
            off += n
        cand_ref[_CAND_TAIL:_CAND_USED, :] = v1_ref[half:, :] + v2_ref[0:1, :]
        for lb in range(tm // LANES):
            cols = slice(lb * LANES, (lb + 1) * LANES)
            _top16_rows(cand_ref[:, cols], code_ref[:, cols], 2 * PEER_TOPK * PEER_TOPK, ts_ref, tp_ref, cols)
        ts = ts_ref[...]
        pos = tp_ref[...]
        ak = jnp.floor(pos * (1.0 / PEER_TOPK))
        bk = pos - PEER_TOPK * ak
        i1 = jnp.zeros_like(ts)
        i2 = jnp.zeros_like(ts)
        for a in range(PEER_TOPK):
            i1 = jnp.where(ak == a, j1_ref[a:a + 1, :], i1)
            i2 = jnp.where(bk == a, j2_ref[a:a + 1, :], i2)
        e = jnp.exp(ts - ts[0:1, :])
        rows = slice(hd * PEER_TOPK, (hd + 1) * PEER_TOPK)
        o1_ref[rows, :] = i1
        o2_ref[rows, :] = i2
        og_ref[rows, :] = e / jnp.sum(e, axis=0, keepdims=True)
    i1_ref[...] = o1_ref[...].T.astype(I32)
    i2_ref[...] = o2_ref[...].T.astype(I32)
    g_ref[...] = og_ref[...].T


def _peer_route(x, sc, sh, w_q, sub_keys, n_lat_tiles):
    rows, d = x.shape
    nq = w_q.shape[1]
    sel = lambda i: (i // n_lat_tiles, 0, 0)
    tile = lambda i: (i, 0)
    codes = jnp.broadcast_to(jnp.asarray(_cand_codes(), F32)[:, None], (N_CAND, TM))
    return pl.pallas_call(
        _peer_route_kernel,
        grid=(rows // TM,),
        in_specs=[pl.BlockSpec((TM, d), tile),
                  pl.BlockSpec((None, NB, d), sel),
                  pl.BlockSpec((None, NB, d), sel),
                  pl.BlockSpec((d, nq), lambda i: (0, 0)),
                  pl.BlockSpec(sub_keys.shape, lambda i: (0, 0, 0)),
                  pl.BlockSpec((N_CAND, TM), lambda i: (0, 0))],
        out_specs=[pl.BlockSpec((TM, d), tile),
                   pl.BlockSpec((TM, N_SLOTS), tile),
                   pl.BlockSpec((TM, N_SLOTS), tile),
                   pl.BlockSpec((TM, N_SLOTS), tile)],
        out_shape=[jax.ShapeDtypeStruct((rows, d), BF16),
                   jax.ShapeDtypeStruct((rows, N_SLOTS), I32),
                   jax.ShapeDtypeStruct((rows, N_SLOTS), I32),
                   jax.ShapeDtypeStruct((rows, N_SLOTS), F32)],
        scratch_shapes=[pltpu.VMEM((PEER_TOPK, TM), F32)] * 6
                       + [pltpu.VMEM((N_CAND, TM), F32)]
                       + [pltpu.VMEM((N_SLOTS, TM), F32)] * 3,
        compiler_params=_cparams(("arbitrary",)),
        name="peer_route",
    )(x, sc, sh, w_q, sub_keys, codes)


def _peer_act_kernel(n_steps, hb_ref, u_ref, i1_ref, i2_ref, g_ref, w_ref, z_ref, sa_ref, sb_ref):
    cj = pl.program_id(1)
    per_step = EXP_CHUNK // N_KEYS

    def scores(dst_ref):
        dst_ref[...] = lax.dot_general(hb_ref[...], u_ref[...], (((1,), (1,)), ((), ())),
                                       preferred_element_type=F32)

    def gather(src_ref):
        i1 = i1_ref[...]
        i2 = i2_ref[...]
        z = z_ref[...]
        for q in range(per_step):
            picked = jnp.take_along_axis(src_ref[:, q * N_KEYS:(q + 1) * N_KEYS], i2, axis=1)
            z = jnp.where(i1 == (cj - 1) * per_step + q, picked, z)
        z_ref[...] = z
        return z

    @pl.when(cj == 0)
    def _():
        z_ref[...] = jnp.zeros_like(z_ref)
        scores(sa_ref)

    steady = jnp.logical_and(cj > 0, cj < n_steps)

    @pl.when(jnp.logical_and(steady, cj % 2 == 1))
    def _():
        gather(sa_ref)
        scores(sb_ref)

    @pl.when(jnp.logical_and(steady, cj % 2 == 0))
    def _():
        gather(sb_ref)
        scores(sa_ref)

    @pl.when(cj == n_steps)
    def _():
        z = gather(sa_ref if n_steps % 2 == 1 else sb_ref)
        w_ref[...] = g_ref[...] * _gelu(z)


def _peer_act(hb, u_tab, i1, i2, g):
    rows, d = hb.shape
    n_exp = u_tab.shape[0]
    n_steps = n_exp // EXP_CHUNK
    tile = lambda i, j: (i, 0)
    return pl.pallas_call(
        functools.partial(_peer_act_kernel, n_steps),
        grid=(rows // TM_ACT, n_steps + 1),
        in_specs=[pl.BlockSpec((TM_ACT, d), tile),
                  pl.BlockSpec((EXP_CHUNK, d), lambda i, j: (jnp.minimum(j, n_steps - 1), 0)),
                  pl.BlockSpec((TM_ACT, N_SLOTS), tile),
                  pl.BlockSpec((TM_ACT, N_SLOTS), tile),
                  pl.BlockSpec((TM_ACT, N_SLOTS), tile)],
        out_specs=pl.BlockSpec((TM_ACT, N_SLOTS), tile),
        out_shape=jax.ShapeDtypeStruct((rows, N_SLOTS), F32),
        scratch_shapes=[pltpu.VMEM((TM_ACT, N_SLOTS), F32),
                        pltpu.VMEM((TM_ACT, EXP_CHUNK), F32),
                        pltpu.VMEM((TM_ACT, EXP_CHUNK), F32)],
        compiler_params=_cparams(("arbitrary", "arbitrary")),
        name="peer_act",
    )(hb, u_tab, i1, i2, g)


def _peer_out_kernel(alpha, i1_ref, i2_ref, w_ref, v_ref, x_ref, g2_ref, lg_ref, lb_ref, o_ref,
                     wbuf_ref, wd_ref, acc_ref):
    vj = pl.program_id(1)
    tm = x_ref.shape[0]
    groups = VAL_CHUNK // N_KEYS

    @pl.when(vj == 0)
    def _():
        acc_ref[...] = jnp.zeros_like(acc_ref)
        key_iota = lax.broadcasted_iota(I32, (N_KEYS, N_SLOTS), 0)
        nt = (((1,), (1,)), ((), ()))

        for sub in range(tm // ASM_TM):
            base = sub * ASM_TM

            def assemble(tb, carry):
                for u in range(ASM_UNROLL):
                    t = tb * ASM_UNROLL + u
                    row = pl.ds(base + t, 1)
                    i1 = jnp.broadcast_to(i1_ref[row, :], (N_KEYS, N_SLOTS))
                    i2 = jnp.broadcast_to(i2_ref[row, :], (N_KEYS, N_SLOTS))
                    w = jnp.broadcast_to(w_ref[row, :], (N_KEYS, N_SLOTS))
                    lhs = jnp.where(i1 == key_iota, w, 0.0).astype(BF16)
                    rhs = jnp.where(i2 == key_iota, 1.0, 0.0).astype(BF16)
                    wbuf_ref[pl.ds(pl.multiple_of(t * W_STRIDE, 8), N_KEYS), :] = lax.dot_general(
                        lhs, rhs, nt, preferred_element_type=F32)
                return carry

            lax.fori_loop(0, ASM_TM // ASM_UNROLL, assemble, 0)

            def regroup(cb, carry):
                for u in range(ASM_UNROLL):
                    c = cb * ASM_UNROLL + u
                    wd_ref[c, base:base + ASM_TM, :] = wbuf_ref[pl.ds(c, ASM_TM, stride=W_STRIDE), :].astype(BF16)
                return carry

            lax.fori_loop(0, N_KEYS // ASM_UNROLL, regroup, 0)

    lhs = jnp.concatenate([wd_ref[vj * groups + q] for q in range(groups)], axis=1)
    acc_ref[...] += jnp.dot(lhs, v_ref[...], preferred_element_type=F32)

    @pl.when(vj == pl.num_programs(1) - 1)
    def _():
        o_ref[...] = _post_norm(x_ref[...], acc_ref[...], g2_ref[...], lg_ref[...], lb_ref[...], alpha)


def _peer_out(i1, i2, w, v_tab, x, g2, lg, lb, n_lat_tiles, alpha):
    rows, d = x.shape
    n_exp = v_tab.shape[0]
    tile = lambda i, j: (i, 0)
    const = lambda i, j: (0, 0)
    return pl.pallas_call(
        functools.partial(_peer_out_kernel, alpha),
        grid=(rows // TM_OUT, n_exp // VAL_CHUNK),
        in_specs=[pl.BlockSpec((TM_OUT, N_SLOTS), tile),
                  pl.BlockSpec((TM_OUT, N_SLOTS), tile),
                  pl.BlockSpec((TM_OUT, N_SLOTS), tile),
                  pl.BlockSpec((VAL_CHUNK, d), lambda i, j: (j, 0)),
                  pl.BlockSpec((TM_OUT, d), tile),
                  pl.BlockSpec((None, NB, d), lambda i, j: (i // n_lat_tiles, 0, 0)),
                  pl.BlockSpec((1, d), const),
                  pl.BlockSpec((1, d), const)],
        out_specs=pl.BlockSpec((TM_OUT, d), tile),
        out_shape=jax.ShapeDtypeStruct((rows, d), F32),
        scratch_shapes=[pltpu.VMEM((ASM_TM * W_STRIDE, N_KEYS), F32),
                        pltpu.VMEM((n_exp // N_KEYS, TM_OUT, N_KEYS), BF16),
                        pltpu.VMEM((TM_OUT, d), F32)],
        compiler_params=_cparams(("arbitrary", "arbitrary")),
        name="peer_out",
    )(i1, i2, w, v_tab, x, g2, lg, lb)


def kernel(x, c, ctx, c_ctx, mod_w, mod_b, ln_g, ln_b, lru_w_in, lru_conv_w, lru_conv_b, lru_gate_w, lru_gate_b,
           lru_lambda, lru_w_out, attn_w_qkv, attn_q_gain, attn_k_gain, attn_w_out, peer_w_q, peer_sub_keys,
           peer_u, peer_v):
    nb, seq, d = x.shape
    ctx_len = ctx.shape[1]
    depth = mod_w.shape[0]
    assert nb == NB and seq % GRID_W == 0 and seq % TQ == 0 and ctx_len % TQ == 0
    for t in (TM, TM_ACT, TM_OUT):
        assert (seq * NB) % t == 0 and (ctx_len * NB) % t == 0
    lt = seq + ctx_len
    n_lat, n_ctx = seq * NB // TM, ctx_len * NB // TM
    alpha = (2 * depth) ** 0.25

    xs = jnp.concatenate([x.transpose(1, 0, 2), ctx.transpose(1, 0, 2)], axis=0).reshape(lt * NB, d)

    cond = jnp.concatenate([c, c_ctx[None], jnp.zeros((7, d), F32)], axis=0)
    mtab = _modulation_table(cond, mod_w, mod_b)
    lat = mtab[:, :NB].reshape(depth, NB, 6, d).transpose(0, 2, 1, 3)
    cx = jnp.broadcast_to(mtab[:, NB].reshape(depth, 6, 1, d), (depth, 6, NB, d))
    mods = jnp.stack([lat, cx], axis=2)

    cos, sa, sb = _rope_tables(seq, ctx_len)
    lam = lru_lambda.reshape(-1, lru_lambda.shape[-1])
    softplus_neg_lam = pl.pallas_call(
        _softplus_kernel, out_shape=jax.ShapeDtypeStruct(lam.shape, F32), name="softplus")(-lam)
    softplus_neg_lam = softplus_neg_lam.reshape(lru_lambda.shape)

    for i in range(depth):
        sh1, sc1, g1, sh2, sc2, g2 = (mods[i, j] for j in range(6))
        lg1, lb1 = ln_g[i, 0][None], ln_b[i, 0][None]
        lg2, lb2 = ln_g[i, 1][None], ln_b[i, 1][None]
        j = i // 2
        if i % 2 == 0:
            gate, xr = _lru_in(xs, sc1, sh1, lru_w_in[j].astype(BF16), n_lat)
            wg0, gb0 = _gate_weights(lru_gate_w[j, 0], lru_gate_b[j, 0])
            wg1, gb1 = _gate_weights(lru_gate_w[j, 1], lru_gate_b[j, 1])
            xc, hf = _lru_fwd(xr, lru_conv_w[j], lru_conv_b[j][None], wg0, gb0, softplus_neg_lam[j, 0][None],
                              n_lat, n_ctx)
            xs = _lru_bwd(xc, hf, gate, xs, g1, wg1, gb1, softplus_neg_lam[j, 1][None],
                          lru_w_out[j].astype(BF16), lg1, lb1, n_lat, n_ctx, alpha)
        else:
            qkv = _modmm(xs, sc1, sh1, attn_w_qkv[j].astype(BF16), n_lat, "qkv")
            q, k, v = _qk_prep(qkv, cos, sa, sb, attn_q_gain[j][None], attn_k_gain[j][None], lt)
            o = _attention(q, k, v, seq).reshape(lt * NB, N_Q_HEADS * HEAD_DIM)
            xs = _attn_out(o, xs, g1, attn_w_out[j].astype(BF16), lg1, lb1, n_lat, alpha)
        hb, i1, i2, g = _peer_route(xs, sc2, sh2, peer_w_q[i].astype(BF16), peer_sub_keys[i].astype(BF16), n_lat)
        w = _peer_act(hb, peer_u[i].astype(BF16), i1, i2, g)
        xs = _peer_out(i1, i2, w, peer_v[i].astype(BF16), xs, g2, lg2, lb2, seq * NB // TM_OUT, alpha)

    return xs.reshape(lt, NB, d)[:seq].transpose(1, 0, 2)
```

```python
import functools

import jax
import jax.numpy as jnp
from jax import lax
from jax.experimental import pallas as pl
from jax.experimental.pallas import tpu as pltpu

F32 = jnp.float32
BF16 = jnp.bfloat16
I32 = jnp.int32

NB = 16
GRID_W = 64
CONV_W = 4
LRU_C = 8.0
N_Q_HEADS = 8
N_KV_HEADS = 2
HEAD_DIM = 128
Q_PER_KV = N_Q_HEADS // N_KV_HEADS
ROPE_THETA = 10000.0
PEER_HEADS = 8
N_KEYS = 128
PEER_TOPK = 16
N_SLOTS = PEER_HEADS * PEER_TOPK
LN_EPS = 1e-6
RMS_EPS = 1e-6
LANES = 128
VMEM_LIMIT = 56 * 1024 * 1024

TM = 256
TT = TM // NB
TQ = 256
TL_PREP = 32
TM_ACT = 1024
EXP_CHUNK = 1024
VAL_CHUNK = 1024
W_STRIDE = N_KEYS + 8
ASM_UNROLL = 32
TM_OUT = 512
ASM_TM = 128


def _cparams(sem):
    return pltpu.CompilerParams(dimension_semantics=sem, vmem_limit_bytes=VMEM_LIMIT)


def _gelu(x):
    return 0.5 * x * (1.0 + lax.erf(x * 0.7071067811865476))


def _modulate(x, sc, sh):
    rows, d = x.shape
    x3 = x.reshape(rows // NB, NB, d)
    return (x3 * (1.0 + sc)[None] + sh[None]).reshape(rows, d)


def _post_norm(x, delta, gate, lg, lb, alpha):
    rows, d = x.shape
    v = alpha * x + (delta.reshape(rows // NB, NB, d) * gate[None]).reshape(rows, d)
    mu = jnp.mean(v, axis=-1, keepdims=True)
    vc = v - mu
    var = jnp.mean(vc * vc, axis=-1, keepdims=True)
    return vc * lax.rsqrt(var + LN_EPS) * lg + lb


def _mod_kernel(c_ref, w_ref, b_ref, o_ref):
    c = c_ref[...]
    s = (c * jax.nn.sigmoid(c)).astype(BF16)
    o_ref[...] = jnp.dot(s, w_ref[...].astype(BF16), preferred_element_type=F32) + b_ref[...]


def _modulation_table(cond, mod_w, mod_b):
    depth, d, d6 = mod_w.shape
    nblk = d6 // d
    return pl.pallas_call(
        _mod_kernel,
        grid=(depth, nblk),
        in_specs=[pl.BlockSpec((cond.shape[0], d), lambda l, j: (0, 0)),
                  pl.BlockSpec((None, d, d), lambda l, j: (l, 0, j)),
                  pl.BlockSpec((None, 1, d), lambda l, j: (l, 0, j))],
        out_specs=pl.BlockSpec((None, cond.shape[0], d), lambda l, j: (l, 0, j)),
        out_shape=jax.ShapeDtypeStruct((depth, cond.shape[0], d6), F32),
        compiler_params=_cparams(("arbitrary", "arbitrary")),
        name="modulation",
    )(cond, mod_w, mod_b.reshape(depth, 1, d6))


def _lru_in_kernel(x_ref, sc_ref, sh_ref, w_ref, gate_ref, xr_ref):
    h = _modulate(x_ref[...], sc_ref[...], sh_ref[...]).astype(BF16)
    u = jnp.dot(h, w_ref[...], preferred_element_type=F32)
    c = gate_ref.shape[-1]
    gate_ref[...] = _gelu(u[:, :c])
    xr_ref[...] = u[:, c:]


def _lru_in(x, sc, sh, w_in, n_lat_tiles):
    rows, d = x.shape
    c2 = w_in.shape[1]
    c = c2 // 2
    sel = lambda i: (i // n_lat_tiles, 0, 0)
    return pl.pallas_call(
        _lru_in_kernel,
        grid=(rows // TM,),
        in_specs=[pl.BlockSpec((TM, d), lambda i: (i, 0)),
                  pl.BlockSpec((None, NB, d), sel),
                  pl.BlockSpec((None, NB, d), sel),
                  pl.BlockSpec((d, c2), lambda i: (0, 0))],
        out_specs=[pl.BlockSpec((TM, c), lambda i: (i, 0)),
                   pl.BlockSpec((TM, c), lambda i: (i, 0))],
        out_shape=[jax.ShapeDtypeStruct((rows, c), F32), jax.ShapeDtypeStruct((rows, c), F32)],
        compiler_params=_cparams(("arbitrary",)),
        name="lru_in",
    )(x, sc, sh, w_in)


def _row_permutation():
    r = jnp.arange(TM)
    src = (r % TT) * NB + r // TT
    return (src[:, None] == jnp.arange(TM)[None, :]).astype(BF16)


def _qkv_kernel(x_ref, sc_ref, sh_ref, perm_ref, w_ref, o_ref):
    h = _modulate(x_ref[...], sc_ref[...], sh_ref[...]).astype(BF16)
    hp = jnp.dot(perm_ref[...], h, preferred_element_type=F32).astype(BF16)
    out = jnp.dot(hp, w_ref[...], preferred_element_type=F32)
    o_ref[...] = out.reshape(NB, TT, out.shape[-1])


def _qkv(x, sc, sh, w, n_lat_tiles):
    rows, d = x.shape
    n = w.shape[1]
    sel = lambda i: (i // n_lat_tiles, 0, 0)
    return pl.pallas_call(
        _qkv_kernel,
        grid=(rows // TM,),
        in_specs=[pl.BlockSpec((TM, d), lambda i: (i, 0)),
                  pl.BlockSpec((None, NB, d), sel),
                  pl.BlockSpec((None, NB, d), sel),
                  pl.BlockSpec((TM, TM), lambda i: (0, 0)),
                  pl.BlockSpec((d, n), lambda i: (0, 0))],
        out_specs=pl.BlockSpec((NB, TT, n), lambda i: (0, i, 0)),
        out_shape=jax.ShapeDtypeStruct((NB, rows // NB, n), F32),
        compiler_params=_cparams(("arbitrary",)),
        name="qkv",
    )(x, sc, sh, _row_permutation(), w)


def _rglru_coeffs(xc, wg_ref, gb_ref, sp_ref, a_ref, b_ref):
    c = xc.shape[-1]
    gates = jnp.dot(xc.astype(BF16), wg_ref[...], preferred_element_type=F32) + gb_ref[...]
    r = jax.nn.sigmoid(gates[:, :c])
    i = jax.nn.sigmoid(gates[:, c:])
    log_a = -LRU_C * r * sp_ref[...]
    a = jnp.exp(log_a)
    a_ref[...] = a
    b_ref[...] = jnp.sqrt(-(a * a + 1.0) * jnp.tanh(log_a)) * i * xc


def _lru_fwd_kernel(n_lat, n_ctx, xr_ref, prev_ref, next_ref, cw_ref, cb_ref, wg_ref, gb_ref, sp_ref,
                    xc_ref, hf_ref, a_ref, b_ref, h_ref):
    g = pl.program_id(0)

    @pl.when(g == 0)
    def _():
        h_ref[...] = jnp.zeros_like(h_ref)

    first = jnp.logical_or(g == 0, g == n_ctx)
    last = jnp.logical_or(g == n_ctx - 1, g == n_ctx + n_lat - 1)
    prev = jnp.where(first, 0.0, prev_ref[...])
    nxt = jnp.where(last, 0.0, next_ref[...])
    ext = jnp.concatenate([prev, xr_ref[...], nxt], axis=0)
    cw = cw_ref[...]
    xc = cb_ref[...] + cw[0:1] * ext[0:TM]
    for j in range(1, CONV_W):
        xc = xc + cw[j:j + 1] * ext[j * NB:j * NB + TM]
    xc_ref[...] = xc
    _rglru_coeffs(xc, wg_ref, gb_ref, sp_ref, a_ref, b_ref)
    h = h_ref[...]
    for t in range(TT):
        rows = pl.ds(t * NB, NB)
        h = a_ref[rows, :] * h + b_ref[rows, :]
        hf_ref[rows, :] = h
    h_ref[...] = h


def _lru_fwd(xr, conv_w, conv_b, wg, gb, sp, n_lat, n_ctx):
    rows, c = xr.shape
    halo_p, halo_n = 2 * NB, NB
    per_p, per_n = TM // halo_p, TM // halo_n
    chunk = lambda g: jnp.where(g < n_ctx, n_lat + g, g - n_ctx)
    const = lambda g: (0, 0)
    return pl.pallas_call(
        functools.partial(_lru_fwd_kernel, n_lat, n_ctx),
        grid=(n_lat + n_ctx,),
        in_specs=[pl.BlockSpec((TM, c), lambda g: (chunk(g), 0)),
                  pl.BlockSpec((halo_p, c), lambda g: (jnp.maximum(chunk(g) * per_p - 1, 0), 0)),
                  pl.BlockSpec((halo_n, c), lambda g: (jnp.minimum((chunk(g) + 1) * per_n, rows // halo_n - 1), 0)),
                  pl.BlockSpec((CONV_W, c), const),
                  pl.BlockSpec((1, c), const),
                  pl.BlockSpec((c, 2 * c), const),
                  pl.BlockSpec((1, 2 * c), const),
                  pl.BlockSpec((1, c), const)],
        out_specs=[pl.BlockSpec((TM, c), lambda g: (chunk(g), 0)),
                   pl.BlockSpec((TM, c), lambda g: (chunk(g), 0))],
        out_shape=[jax.ShapeDtypeStruct((rows, c), F32), jax.ShapeDtypeStruct((rows, c), F32)],
        scratch_shapes=[pltpu.VMEM((TM, c), F32), pltpu.VMEM((TM, c), F32), pltpu.VMEM((NB, c), F32)],
        compiler_params=_cparams(("arbitrary",)),
        name="lru_fwd",
    )(xr, xr, xr, conv_w, conv_b, wg, gb, sp)


def _lru_bwd_kernel(alpha, xc_ref, hf_ref, gate_ref, x_ref, g1_ref, wg_ref, gb_ref, sp_ref, wo_ref,
                    lg_ref, lb_ref, o_ref, a_ref, b_ref, y_ref, h_ref):
    g = pl.program_id(0)

    @pl.when(g == 0)
    def _():
        h_ref[...] = jnp.zeros_like(h_ref)

    _rglru_coeffs(xc_ref[...], wg_ref, gb_ref, sp_ref, a_ref, b_ref)
    h = h_ref[...]
    for t in range(TT - 1, -1, -1):
        rows = pl.ds(t * NB, NB)
        h = a_ref[rows, :] * h + b_ref[rows, :]
        y_ref[rows, :] = h
    h_ref[...] = h
    y = ((hf_ref[...] + y_ref[...]) * gate_ref[...]).astype(BF16)
    delta = jnp.dot(y, wo_ref[...], preferred_element_type=F32)
    o_ref[...] = _post_norm(x_ref[...], delta, g1_ref[...], lg_ref[...], lb_ref[...], alpha)


def _lru_bwd(xc, hf, gate, x, g1, wg, gb, sp, w_out, lg, lb, n_lat, n_ctx, alpha):
    rows, c = xc.shape
    d = x.shape[1]
    chunk = lambda g: jnp.where(g < n_ctx, n_lat + (n_ctx - 1 - g), n_lat - 1 - (g - n_ctx))
    tile = lambda g: (chunk(g), 0)
    const = lambda g: (0, 0)
    return pl.pallas_call(
        functools.partial(_lru_bwd_kernel, alpha),
        grid=(n_lat + n_ctx,),
        in_specs=[pl.BlockSpec((TM, c), tile),
                  pl.BlockSpec((TM, c), tile),
                  pl.BlockSpec((TM, c), tile),
                  pl.BlockSpec((TM, d), tile),
                  pl.BlockSpec((None, NB, d), lambda g: (jnp.where(g < n_ctx, 1, 0), 0, 0)),
                  pl.BlockSpec((c, 2 * c), const),
                  pl.BlockSpec((1, 2 * c), const),
                  pl.BlockSpec((1, c), const),
                  pl.BlockSpec((c, d), const),
                  pl.BlockSpec((1, d), const),
                  pl.BlockSpec((1, d), const)],
        out_specs=pl.BlockSpec((TM, d), tile),
        out_shape=jax.ShapeDtypeStruct((rows, d), F32),
        scratch_shapes=[pltpu.VMEM((TM, c), F32), pltpu.VMEM((TM, c), F32), pltpu.VMEM((TM, c), F32),
                        pltpu.VMEM((NB, c), F32)],
        compiler_params=_cparams(("arbitrary",)),
        name="lru_bwd",
    )(xc, hf, gate, x, g1, wg, gb, sp, w_out, lg, lb)


def _softplus_kernel(x_ref, o_ref):
    x = x_ref[...]
    o_ref[...] = jax.nn.softplus(x)


def _gate_weights(gate_w, gate_b):
    nblk, blk, _ = gate_w.shape
    c = nblk * blk
    eye = jnp.eye(nblk, dtype=gate_w.dtype)
    wr = jnp.einsum('nde,nm->ndme', gate_w[..., :blk], eye).reshape(c, c)
    wi = jnp.einsum('nde,nm->ndme', gate_w[..., blk:], eye).reshape(c, c)
    w = jnp.concatenate([wr, wi], axis=1).astype(BF16)
    b = jnp.concatenate([gate_b[:, :blk].reshape(1, c), gate_b[:, blk:].reshape(1, c)], axis=1)
    return w, b


def _qk_prep_kernel(qkv_ref, cos_ref, sa_ref, sb_ref, qg_ref, kg_ref, q_ref, k_ref, v_ref):
    cos, sa, sb = cos_ref[...], sa_ref[...], sb_ref[...]
    nq = N_Q_HEADS * HEAD_DIM
    nk = N_KV_HEADS * HEAD_DIM

    def norm_rope(xh, gain):
        ms = jnp.mean(xh * xh, axis=-1, keepdims=True)
        xn = xh * lax.rsqrt(ms + RMS_EPS) * gain
        return xn * cos + pltpu.roll(xn, 96, axis=1) * sa + pltpu.roll(xn, 32, axis=1) * sb

    for h in range(N_Q_HEADS):
        sl = slice(h * HEAD_DIM, (h + 1) * HEAD_DIM)
        q_ref[:, sl] = norm_rope(qkv_ref[:, sl], qg_ref[...]).astype(BF16)
    for h in range(N_KV_HEADS):
        sl = slice(h * HEAD_DIM, (h + 1) * HEAD_DIM)
        k_ref[:, sl] = norm_rope(qkv_ref[:, nq + h * HEAD_DIM:nq + (h + 1) * HEAD_DIM], kg_ref[...]).astype(BF16)
    v_ref[...] = qkv_ref[:, nq + nk:].astype(BF16)


def _qk_prep(qkv, cos, sa, sb, qg, kg):
    nb, lt, w = qkv.shape
    nq = N_Q_HEADS * HEAD_DIM
    nk = N_KV_HEADS * HEAD_DIM
    tl = TQ
    const = lambda b, i: (0, 0)
    tab = pl.BlockSpec((tl, HEAD_DIM), lambda b, i: (i, 0))
    return pl.pallas_call(
        _qk_prep_kernel,
        grid=(nb, lt // tl),
        in_specs=[pl.BlockSpec((None, tl, w), lambda b, i: (b, i, 0)), tab, tab, tab,
                  pl.BlockSpec((1, HEAD_DIM), const),
                  pl.BlockSpec((1, HEAD_DIM), const)],
        out_specs=[pl.BlockSpec((None, tl, nq), lambda b, i: (b, i, 0)),
                   pl.BlockSpec((None, tl, nk), lambda b, i: (b, i, 0)),
                   pl.BlockSpec((None, tl, nk), lambda b, i: (b, i, 0))],
        out_shape=[jax.ShapeDtypeStruct((nb, lt, nq), BF16),
                   jax.ShapeDtypeStruct((nb, lt, nk), BF16),
                   jax.ShapeDtypeStruct((nb, lt, nk), BF16)],
        compiler_params=_cparams(("arbitrary", "arbitrary")),
        name="qk_prep",
    )(qkv, cos, sa, sb, qg, kg)


def _attn_kernel(n_lat_tiles, seq, q_ref, k_ref, v_ref, o_ref):
    qi = pl.program_id(2)
    tq = q_ref.shape[0]
    scale = HEAD_DIM ** -0.5

    def attend(k, v):
        for j in range(Q_PER_KV):
            cols = slice(j * HEAD_DIM, (j + 1) * HEAD_DIM)
            s = lax.dot_general(q_ref[:, cols], k, (((1,), (1,)), ((), ())), preferred_element_type=F32) * scale
            m = jnp.max(s, axis=-1, keepdims=True)
            p = jnp.exp(s - m)
            l = jnp.sum(p, axis=-1, keepdims=True)
            o = jnp.dot(p.astype(BF16), v, preferred_element_type=F32) / l
            o_ref[:, cols] = o.astype(BF16)

    @pl.when(qi < n_lat_tiles)
    def _():
        attend(k_ref[...], v_ref[...])

    @pl.when(qi >= n_lat_tiles)
    def _():
        attend(k_ref[seq:, :], v_ref[seq:, :])


def _attention(q, k, v, seq, need_ctx):
    nb, lt, nq = q.shape
    tq = TQ
    gw = Q_PER_KV * HEAD_DIM
    return pl.pallas_call(
        functools.partial(_attn_kernel, seq // tq, seq),
        grid=(nb, N_KV_HEADS, (lt if need_ctx else seq) // tq),
        in_specs=[pl.BlockSpec((None, tq, gw), lambda b, h, i: (b, i, h)),
                  pl.BlockSpec((None, lt, HEAD_DIM), lambda b, h, i: (b, 0, h)),
                  pl.BlockSpec((None, lt, HEAD_DIM), lambda b, h, i: (b, 0, h))],
        out_specs=pl.BlockSpec((None, tq, gw), lambda b, h, i: (b, i, h)),
        out_shape=jax.ShapeDtypeStruct((nb, lt, nq), BF16),
        compiler_params=_cparams(("arbitrary", "arbitrary", "arbitrary")),
        name="attention",
    )(q, k, v)


def _attn_out_kernel(alpha, o_ref, x_ref, g1_ref, perm_ref, w_ref, lg_ref, lb_ref, out_ref):
    o_bt = o_ref[...].reshape(TM, o_ref.shape[-1])
    o_tb = jnp.dot(perm_ref[...], o_bt, preferred_element_type=F32).astype(BF16)
    delta = jnp.dot(o_tb, w_ref[...], preferred_element_type=F32)
    out_ref[...] = _post_norm(x_ref[...], delta, g1_ref[...], lg_ref[...], lb_ref[...], alpha)


def _attn_out(o, x, g1, w_out, lg, lb, n_lat_tiles, rows_out, alpha):
    d = x.shape[1]
    nq = w_out.shape[0]
    const = lambda i: (0, 0)
    return pl.pallas_call(
        functools.partial(_attn_out_kernel, alpha),
        grid=(rows_out // TM,),
        in_specs=[pl.BlockSpec((NB, TT, nq), lambda i: (0, i, 0)),
                  pl.BlockSpec((TM, d), lambda i: (i, 0)),
                  pl.BlockSpec((None, NB, d), lambda i: (i // n_lat_tiles, 0, 0)),
                  pl.BlockSpec((TM, TM), const),
                  pl.BlockSpec((nq, d), const),
                  pl.BlockSpec((1, d), const),
                  pl.BlockSpec((1, d), const)],
        out_specs=pl.BlockSpec((TM, d), lambda i: (i, 0)),
        out_shape=jax.ShapeDtypeStruct((rows_out, d), F32),
        compiler_params=_cparams(("arbitrary",)),
        name="attn_out",
    )(o, x, g1, _row_permutation().T, w_out, lg, lb)


def _rope_tables(seq, ctx_len):
    rows = seq // GRID_W
    row, col = jnp.meshgrid(jnp.arange(rows), jnp.arange(GRID_W), indexing='ij')
    pos = jnp.stack([row.reshape(-1), col.reshape(-1)], axis=-1).astype(F32)
    axis_dim = HEAD_DIM // 2
    inv_freq = ROPE_THETA ** (-jnp.arange(0, axis_dim, 2, dtype=F32) / axis_dim)
    ang = pos[:, :, None] * inv_freq
    cos, sin = jnp.cos(ang), jnp.sin(ang)
    zero = jnp.zeros_like(sin)
    cosf = jnp.concatenate([cos, cos], axis=-1).reshape(seq, HEAD_DIM)
    sa = jnp.concatenate([-sin, zero], axis=-1).reshape(seq, HEAD_DIM)
    sb = jnp.concatenate([zero, sin], axis=-1).reshape(seq, HEAD_DIM)
    pad = lambda t, v: jnp.concatenate([t, jnp.full((ctx_len, HEAD_DIM), v, F32)], axis=0)
    return pad(cosf, 1.0), pad(sa, 0.0), pad(sb, 0.0)


def _top16_rows(s, code, n, val_ref, idx_ref, cols):
    for r in range(PEER_TOPK):
        m = jnp.max(s, axis=0, keepdims=True)
        idx = jnp.min(jnp.where(s == m, code, float(n)), axis=0, keepdims=True)
        s = jnp.where(code == idx, -jnp.inf, s)
        val_ref[r:r + 1, cols] = m
        idx_ref[r:r + 1, cols] = idx


def _top16_keys(s, val_ref, idx_ref, cols):
    half = N_KEYS // 2
    a, b = s[:half], s[half:]
    row = lax.broadcasted_iota(I32, a.shape, 0).astype(F32)
    swap = b > a
    hi, lo = jnp.maximum(a, b), jnp.minimum(a, b)
    hi_idx = jnp.where(swap, row + half, row)
    lo_idx = jnp.where(swap, row, row + half)
    for r in range(PEER_TOPK):
        m = jnp.max(hi, axis=0, keepdims=True)
        idx = jnp.min(jnp.where(hi == m, hi_idx, float(N_KEYS)), axis=0, keepdims=True)
        taken = hi_idx == idx
        hi = jnp.where(taken, lo, hi)
        hi_idx = jnp.where(taken, lo_idx, hi_idx)
        lo = jnp.where(taken, -jnp.inf, lo)
        val_ref[r:r + 1, cols] = m
        idx_ref[r:r + 1, cols] = idx


_CAND_WIDE = [(a, PEER_TOPK // (a + 1)) for a in range(PEER_TOPK // 2)]
_CAND_TAIL = sum(n for _, n in _CAND_WIDE)
_CAND_USED = _CAND_TAIL + PEER_TOPK // 2
N_CAND = -(-_CAND_USED // 8) * 8


def _cand_codes():
    codes = [a * PEER_TOPK + b for a, n in _CAND_WIDE for b in range(n)]
    codes += [a * PEER_TOPK for a in range(PEER_TOPK // 2, PEER_TOPK)]
    codes += [PEER_TOPK * PEER_TOPK + i for i in range(N_CAND - len(codes))]
    return codes


def _peer_route_kernel(x_ref, sc_ref, sh_ref, wq_ref, sk_ref, code_ref, hb_ref, i1_ref, i2_ref, g_ref,
                       v1_ref, j1_ref, v2_ref, j2_ref, ts_ref, tp_ref, cand_ref, o1_ref, o2_ref, og_ref):
    h = _modulate(x_ref[...], sc_ref[...], sh_ref[...]).astype(BF16)
    hb_ref[...] = h
    hq = jnp.dot(h, wq_ref[...], preferred_element_type=F32).astype(BF16)
    tm = h.shape[0]
    nt = (((1,), (1,)), ((), ()))
    half = PEER_TOPK // 2
    cand_ref[_CAND_USED:, :] = jnp.full((N_CAND - _CAND_USED, tm), -jnp.inf, F32)
    for hd in range(PEER_HEADS):
        for p, (v_ref, j_ref) in enumerate(((v1_ref, j1_ref), (v2_ref, j2_ref))):
            qs = hq[:, (hd * 2 + p) * LANES:(hd * 2 + p + 1) * LANES]
            s = lax.dot_general(sk_ref[p], qs, nt, preferred_element_type=F32)
            for lb in range(tm // LANES):
                cols = slice(lb * LANES, (lb + 1) * LANES)
                _top16_keys(s[:, cols], v_ref, j_ref, cols)
        off = 0
        for a, n in _CAND_WIDE:
            cand_ref[off:off + n, :] = v1_ref[a:a + 1, :] + v2_ref[0:n, :]
            off += n
        cand_ref[_CAND_TAIL:_CAND_USED, :] = v1_ref[half:, :] + v2_ref[0:1, :]
        for lb in range(tm // LANES):
            cols = slice(lb * LANES, (lb + 1) * LANES)
            _top16_rows(cand_ref[:, cols], code_ref[:, cols], 2 * PEER_TOPK * PEER_TOPK, ts_ref, tp_ref, cols)
        ts = ts_ref[...]
        pos = tp_ref[...]
        ak = jnp.floor(pos * (1.0 / PEER_TOPK))
        bk = pos - PEER_TOPK * ak
        i1 = jnp.zeros_like(ts)
        i2 = jnp.zeros_like(ts)
        for a in range(PEER_TOPK):
            i1 = jnp.where(ak == a, j1_ref[a:a + 1, :], i1)
            i2 = jnp.where(bk == a, j2_ref[a:a + 1, :], i2)
        e = jnp.exp(ts - ts[0:1, :])
        rows = slice(hd * PEER_TOPK, (hd + 1) * PEER_TOPK)
        o1_ref[rows, :] = i1
        o2_ref[rows, :] = i2
        og_ref[rows, :] = e / jnp.sum(e, axis=0, keepdims=True)
    i1_ref[...] = o1_ref[...].T.astype(I32)
    i2_ref[...] = o2_ref[...].T.astype(I32)
    g_ref[...] = og_ref[...].T


def _peer_route(x, sc, sh, w_q, sub_keys, n_lat_tiles):
    rows, d = x.shape
    nq = w_q.shape[1]
    sel = lambda i: (i // n_lat_tiles, 0, 0)
    tile = lambda i: (i, 0)
    codes = jnp.broadcast_to(jnp.asarray(_cand_codes(), F32)[:, None], (N_CAND, TM))
    return pl.pallas_call(
        _peer_route_kernel,
        grid=(rows // TM,),
        in_specs=[pl.BlockSpec((TM, d), tile),
                  pl.BlockSpec((None, NB, d), sel),
                  pl.BlockSpec((None, NB, d), sel),
                  pl.BlockSpec((d, nq), lambda i: (0, 0)),
                  pl.BlockSpec(sub_keys.shape, lambda i: (0, 0, 0)),
                  pl.BlockSpec((N_CAND, TM), lambda i: (0, 0))],
        out_specs=[pl.BlockSpec((TM, d), tile),
                   pl.BlockSpec((TM, N_SLOTS), tile),
                   pl.BlockSpec((TM, N_SLOTS), tile),
                   pl.BlockSpec((TM, N_SLOTS), tile)],
        out_shape=[jax.ShapeDtypeStruct((rows, d), BF16),
                   jax.ShapeDtypeStruct((rows, N_SLOTS), I32),
                   jax.ShapeDtypeStruct((rows, N_SLOTS), I32),
                   jax.ShapeDtypeStruct((rows, N_SLOTS), F32)],
        scratch_shapes=[pltpu.VMEM((PEER_TOPK, TM), F32)] * 6
                       + [pltpu.VMEM((N_CAND, TM), F32)]
                       + [pltpu.VMEM((N_SLOTS, TM), F32)] * 3,
        compiler_params=_cparams(("arbitrary",)),
        name="peer_route",
    )(x, sc, sh, w_q, sub_keys, codes)


def _peer_act_kernel(n_steps, hb_ref, u_ref, i1_ref, i2_ref, g_ref, w_ref, z_ref, sa_ref, sb_ref):
    cj = pl.program_id(1)
    per_step = EXP_CHUNK // N_KEYS

    def scores(dst_ref):
        dst_ref[...] = lax.dot_general(hb_ref[...], u_ref[...], (((1,), (1,)), ((), ())),
                                       preferred_element_type=F32)

    def gather(src_ref):
        i1 = i1_ref[...]
        i2 = i2_ref[...]
        z = z_ref[...]
        for q in range(per_step):
            picked = jnp.take_along_axis(src_ref[:, q * N_KEYS:(q + 1) * N_KEYS], i2, axis=1)
            z = jnp.where(i1 == (cj - 1) * per_step + q, picked, z)
        z_ref[...] = z
        return z

    @pl.when(cj == 0)
    def _():
        z_ref[...] = jnp.zeros_like(z_ref)
        scores(sa_ref)

    steady = jnp.logical_and(cj > 0, cj < n_steps)

    @pl.when(jnp.logical_and(steady, cj % 2 == 1))
    def _():
        gather(sa_ref)
        scores(sb_ref)

    @pl.when(jnp.logical_and(steady, cj % 2 == 0))
    def _():
        gather(sb_ref)
        scores(sa_ref)

    @pl.when(cj == n_steps)
    def _():
        z = gather(sa_ref if n_steps % 2 == 1 else sb_ref)
        w_ref[...] = g_ref[...] * _gelu(z)


def _peer_act(hb, u_tab, i1, i2, g):
    rows, d = hb.shape
    n_exp = u_tab.shape[0]
    n_steps = n_exp // EXP_CHUNK
    tile = lambda i, j: (i, 0)
    return pl.pallas_call(
        functools.partial(_peer_act_kernel, n_steps),
        grid=(rows // TM_ACT, n_steps + 1),
        in_specs=[pl.BlockSpec((TM_ACT, d), tile),
                  pl.BlockSpec((EXP_CHUNK, d), lambda i, j: (jnp.minimum(j, n_steps - 1), 0)),
                  pl.BlockSpec((TM_ACT, N_SLOTS), tile),
                  pl.BlockSpec((TM_ACT, N_SLOTS), tile),
                  pl.BlockSpec((TM_ACT, N_SLOTS), tile)],
        out_specs=pl.BlockSpec((TM_ACT, N_SLOTS), tile),
        out_shape=jax.ShapeDtypeStruct((rows, N_SLOTS), F32),
        scratch_shapes=[pltpu.VMEM((TM_ACT, N_SLOTS), F32),
                        pltpu.VMEM((TM_ACT, EXP_CHUNK), F32),
                        pltpu.VMEM((TM_ACT, EXP_CHUNK), F32)],
        compiler_params=_cparams(("arbitrary", "arbitrary")),
        name="peer_act",
    )(hb, u_tab, i1, i2, g)


def _peer_out_kernel(alpha, i1_ref, i2_ref, w_ref, v_ref, x_ref, g2_ref, lg_ref, lb_ref, o_ref,
                     wbuf_ref, wd_ref, acc_ref):
    vj = pl.program_id(1)
    tm = x_ref.shape[0]
    groups = VAL_CHUNK // N_KEYS

    @pl.when(vj == 0)
    def _():
        acc_ref[...] = jnp.zeros_like(acc_ref)
        key_iota = lax.broadcasted_iota(I32, (N_KEYS, N_SLOTS), 0)
        nt = (((1,), (1,)), ((), ()))

        for sub in range(tm // ASM_TM):
            base = sub * ASM_TM

            def one_hots(t):
                row = pl.ds(base + t, 1)
                i1 = jnp.broadcast_to(i1_ref[row, :], (N_KEYS, N_SLOTS))
                i2 = jnp.broadcast_to(i2_ref[row, :], (N_KEYS, N_SLOTS))
                w = jnp.broadcast_to(w_ref[row, :], (N_KEYS, N_SLOTS))
                return (jnp.where(i1 == key_iota, w, 0.0).astype(BF16),
                        jnp.where(i2 == key_iota, 1.0, 0.0).astype(BF16))

            def assemble(tb, carry):
                for u in range(ASM_UNROLL):
                    t = tb * ASM_UNROLL + u
                    lhs, rhs = one_hots(t)
                    wbuf_ref[pl.ds(pl.multiple_of(t * W_STRIDE, 8), N_KEYS), :] = lax.dot_general(
                        lhs, rhs, nt, preferred_element_type=F32)
                return carry

            lax.fori_loop(0, ASM_TM // ASM_UNROLL, assemble, 0)

            def regroup(cb, carry):
                for u in range(ASM_UNROLL):
                    c = cb * ASM_UNROLL + u
                    wd_ref[c, base:base + ASM_TM, :] = wbuf_ref[pl.ds(c, ASM_TM, stride=W_STRIDE), :].astype(BF16)
                return carry

            lax.fori_loop(0, N_KEYS // ASM_UNROLL, regroup, 0)

    lhs = jnp.concatenate([wd_ref[vj * groups + q] for q in range(groups)], axis=1)
    acc_ref[...] += jnp.dot(lhs, v_ref[...], preferred_element_type=F32)

    @pl.when(vj == pl.num_programs(1) - 1)
    def _():
        o_ref[...] = _post_norm(x_ref[...], acc_ref[...], g2_ref[...], lg_ref[...], lb_ref[...], alpha)


def _peer_out(i1, i2, w, v_tab, x, g2, lg, lb, n_lat_tiles, alpha):
    rows, d = x.shape
    n_exp = v_tab.shape[0]
    tile = lambda i, j: (i, 0)
    const = lambda i, j: (0, 0)
    return pl.pallas_call(
        functools.partial(_peer_out_kernel, alpha),
        grid=(rows // TM_OUT, n_exp // VAL_CHUNK),
        in_specs=[pl.BlockSpec((TM_OUT, N_SLOTS), tile),
                  pl.BlockSpec((TM_OUT, N_SLOTS), tile),
                  pl.BlockSpec((TM_OUT, N_SLOTS), tile),
                  pl.BlockSpec((VAL_CHUNK, d), lambda i, j: (j, 0)),
                  pl.BlockSpec((TM_OUT, d), tile),
                  pl.BlockSpec((None, NB, d), lambda i, j: (i // n_lat_tiles, 0, 0)),
                  pl.BlockSpec((1, d), const),
                  pl.BlockSpec((1, d), const)],
        out_specs=pl.BlockSpec((TM_OUT, d), tile),
        out_shape=jax.ShapeDtypeStruct((rows, d), F32),
        scratch_shapes=[pltpu.VMEM((ASM_TM * W_STRIDE, N_KEYS), F32),
                        pltpu.VMEM((n_exp // N_KEYS, TM_OUT, N_KEYS), BF16),
                        pltpu.VMEM((TM_OUT, d), F32)],
        compiler_params=_cparams(("arbitrary", "arbitrary")),
        name="peer_out",
    )(i1, i2, w, v_tab, x, g2, lg, lb)


def kernel(x, c, ctx, c_ctx, mod_w, mod_b, ln_g, ln_b, lru_w_in, lru_conv_w, lru_conv_b, lru_gate_w, lru_gate_b,
           lru_lambda, lru_w_out, attn_w_qkv, attn_q_gain, attn_k_gain, attn_w_out, peer_w_q, peer_sub_keys,
           peer_u, peer_v):
    nb, seq, d = x.shape
    ctx_len = ctx.shape[1]
    depth = mod_w.shape[0]
    assert nb == NB and seq % GRID_W == 0 and seq % TQ == 0 and ctx_len % TQ == 0 and (seq + ctx_len) % TL_PREP == 0
    for t in (TM, TM_ACT, TM_OUT):
        assert (seq * NB) % t == 0 and (ctx_len * NB) % t == 0
    lt = seq + ctx_len
    n_lat, n_ctx = seq * NB // TM, ctx_len * NB // TM
    alpha = (2 * depth) ** 0.25

    xs = jnp.concatenate([x.transpose(1, 0, 2), ctx.transpose(1, 0, 2)], axis=0).reshape(lt * NB, d)

    cond = jnp.concatenate([c, c_ctx[None], jnp.zeros((7, d), F32)], axis=0)
    mtab = _modulation_table(cond, mod_w, mod_b)
    lat = mtab[:, :NB].reshape(depth, NB, 6, d).transpose(0, 2, 1, 3)
    cx = jnp.broadcast_to(mtab[:, NB].reshape(depth, 6, 1, d), (depth, 6, NB, d))
    mods = jnp.stack([lat, cx], axis=2)

    cos, sa, sb = _rope_tables(seq, ctx_len)
    lam = lru_lambda.reshape(-1, lru_lambda.shape[-1])
    softplus_neg_lam = pl.pallas_call(
        _softplus_kernel, out_shape=jax.ShapeDtypeStruct(lam.shape, F32), name="softplus")(-lam)
    softplus_neg_lam = softplus_neg_lam.reshape(lru_lambda.shape)

    for i in range(depth):
        sh1, sc1, g1, sh2, sc2, g2 = (mods[i, j] for j in range(6))
        lg1, lb1 = ln_g[i, 0][None], ln_b[i, 0][None]
        lg2, lb2 = ln_g[i, 1][None], ln_b[i, 1][None]
        j = i // 2
        if i % 2 == 0:
            gate, xr = _lru_in(xs, sc1, sh1, lru_w_in[j].astype(BF16), n_lat)
            wg0, gb0 = _gate_weights(lru_gate_w[j, 0], lru_gate_b[j, 0])
            wg1, gb1 = _gate_weights(lru_gate_w[j, 1], lru_gate_b[j, 1])
            xc, hf = _lru_fwd(xr, lru_conv_w[j], lru_conv_b[j][None], wg0, gb0, softplus_neg_lam[j, 0][None],
                              n_lat, n_ctx)
            xs = _lru_bwd(xc, hf, gate, xs, g1, wg1, gb1, softplus_neg_lam[j, 1][None],
                          lru_w_out[j].astype(BF16), lg1, lb1, n_lat, n_ctx, alpha)
        else:
            qkv = _qkv(xs, sc1, sh1, attn_w_qkv[j].astype(BF16), n_lat)
            q, k, v = _qk_prep(qkv, cos, sa, sb, attn_q_gain[j][None], attn_k_gain[j][None])
            need_ctx = i < depth - 1
            o = _attention(q, k, v, seq, need_ctx)
            xs = _attn_out(o, xs, g1, attn_w_out[j].astype(BF16), lg1, lb1, n_lat,
                           (lt if need_ctx else seq) * NB, alpha)
        hb, i1, i2, g = _peer_route(xs, sc2, sh2, peer_w_q[i].astype(BF16), peer_sub_keys[i].astype(BF16), n_lat)
        w = _peer_act(hb, peer_u[i].astype(BF16), i1, i2, g)
        xs = _peer_out(i1, i2, w, peer_v[i].astype(BF16), xs, g2, lg2, lb2, seq * NB // TM_OUT, alpha)

    return xs.reshape(-1, NB, d)[:seq].transpose(1, 0, 2)
```

```python
import functools

import jax
import jax.numpy as jnp
from jax import lax
from jax.experimental import pallas as pl
from jax.experimental.pallas import tpu as pltpu

F32 = jnp.float32
BF16 = jnp.bfloat16
I32 = jnp.int32

NB = 16
GRID_W = 64
CONV_W = 4
LRU_C = 8.0
N_Q_HEADS = 8
N_KV_HEADS = 2
HEAD_DIM = 128
Q_PER_KV = N_Q_HEADS // N_KV_HEADS
ROPE_THETA = 10000.0
PEER_HEADS = 8
N_KEYS = 128
PEER_TOPK = 16
N_SLOTS = PEER_HEADS * PEER_TOPK
LN_EPS = 1e-6
RMS_EPS = 1e-6
LANES = 128
VMEM_LIMIT = 56 * 1024 * 1024

TM = 256
TT = TM // NB
TQ = 256
TL_PREP = 32
TM_ACT = 1024
EXP_CHUNK = 1024
VAL_CHUNK = 2048
W_STRIDE = N_KEYS + 8
ASM_UNROLL = 32
TM_OUT = 512
ASM_TM = 128


def _cparams(sem):
    return pltpu.CompilerParams(dimension_semantics=sem, vmem_limit_bytes=VMEM_LIMIT)


def _gelu(x):
    return 0.5 * x * (1.0 + lax.erf(x * 0.7071067811865476))


def _modulate(x, sc, sh):
    rows, d = x.shape
    x3 = x.reshape(rows // NB, NB, d)
    return (x3 * (1.0 + sc)[None] + sh[None]).reshape(rows, d)


def _post_norm(x, delta, gate, lg, lb, alpha):
    rows, d = x.shape
    v = alpha * x + (delta.reshape(rows // NB, NB, d) * gate[None]).reshape(rows, d)
    mu = jnp.mean(v, axis=-1, keepdims=True)
    vc = v - mu
    var = jnp.mean(vc * vc, axis=-1, keepdims=True)
    return vc * lax.rsqrt(var + LN_EPS) * lg + lb


def _mod_kernel(c_ref, w_ref, b_ref, o_ref):
    c = c_ref[...]
    s = (c * jax.nn.sigmoid(c)).astype(BF16)
    o_ref[...] = jnp.dot(s, w_ref[...].astype(BF16), preferred_element_type=F32) + b_ref[...]


def _modulation_table(cond, mod_w, mod_b):
    depth, d, d6 = mod_w.shape
    nblk = d6 // d
    return pl.pallas_call(
        _mod_kernel,
        grid=(depth, nblk),
        in_specs=[pl.BlockSpec((cond.shape[0], d), lambda l, j: (0, 0)),
                  pl.BlockSpec((None, d, d), lambda l, j: (l, 0, j)),
                  pl.BlockSpec((None, 1, d), lambda l, j: (l, 0, j))],
        out_specs=pl.BlockSpec((None, cond.shape[0], d), lambda l, j: (l, 0, j)),
        out_shape=jax.ShapeDtypeStruct((depth, cond.shape[0], d6), F32),
        compiler_params=_cparams(("arbitrary", "arbitrary")),
        name="modulation",
    )(cond, mod_w, mod_b.reshape(depth, 1, d6))


def _lru_in_kernel(x_ref, sc_ref, sh_ref, w_ref, gate_ref, xr_ref):
    h = _modulate(x_ref[...], sc_ref[...], sh_ref[...]).astype(BF16)
    u = jnp.dot(h, w_ref[...], preferred_element_type=F32)
    c = gate_ref.shape[-1]
    gate_ref[...] = _gelu(u[:, :c])
    xr_ref[...] = u[:, c:]


def _lru_in(x, sc, sh, w_in, n_lat_tiles):
    rows, d = x.shape
    c2 = w_in.shape[1]
    c = c2 // 2
    sel = lambda i: (i // n_lat_tiles, 0, 0)
    return pl.pallas_call(
        _lru_in_kernel,
        grid=(rows // TM,),
        in_specs=[pl.BlockSpec((TM, d), lambda i: (i, 0)),
                  pl.BlockSpec((None, NB, d), sel),
                  pl.BlockSpec((None, NB, d), sel),
                  pl.BlockSpec((d, c2), lambda i: (0, 0))],
        out_specs=[pl.BlockSpec((TM, c), lambda i: (i, 0)),
                   pl.BlockSpec((TM, c), lambda i: (i, 0))],
        out_shape=[jax.ShapeDtypeStruct((rows, c), F32), jax.ShapeDtypeStruct((rows, c), F32)],
        compiler_params=_cparams(("arbitrary",)),
        name="lru_in",
    )(x, sc, sh, w_in)


def _row_permutation():
    r = jnp.arange(TM)
    src = (r % TT) * NB + r // TT
    return (src[:, None] == jnp.arange(TM)[None, :]).astype(BF16)


def _qkv_kernel(x_ref, sc_ref, sh_ref, perm_ref, w_ref, o_ref):
    h = _modulate(x_ref[...], sc_ref[...], sh_ref[...]).astype(BF16)
    hp = jnp.dot(perm_ref[...], h, preferred_element_type=F32).astype(BF16)
    out = jnp.dot(hp, w_ref[...], preferred_element_type=F32)
    o_ref[...] = out.reshape(NB, TT, out.shape[-1])


def _qkv(x, sc, sh, w, n_lat_tiles):
    rows, d = x.shape
    n = w.shape[1]
    sel = lambda i: (i // n_lat_tiles, 0, 0)
    return pl.pallas_call(
        _qkv_kernel,
        grid=(rows // TM,),
        in_specs=[pl.BlockSpec((TM, d), lambda i: (i, 0)),
                  pl.BlockSpec((None, NB, d), sel),
                  pl.BlockSpec((None, NB, d), sel),
                  pl.BlockSpec((TM, TM), lambda i: (0, 0)),
                  pl.BlockSpec((d, n), lambda i: (0, 0))],
        out_specs=pl.BlockSpec((NB, TT, n), lambda i: (0, i, 0)),
        out_shape=jax.ShapeDtypeStruct((NB, rows // NB, n), F32),
        compiler_params=_cparams(("arbitrary",)),
        name="qkv",
    )(x, sc, sh, _row_permutation(), w)


def _rglru_coeffs(xc, wg_ref, gb_ref, sp_ref, a_ref, b_ref):
    c = xc.shape[-1]
    groups, width, _ = wg_ref.shape
    xb = xc.astype(BF16)
    parts = [jnp.dot(xb[:, s * width:(s + 1) * width], wg_ref[s], preferred_element_type=F32)
             for s in range(groups)]
    r = jax.nn.sigmoid(jnp.concatenate([p[:, :width] for p in parts], axis=1) + gb_ref[:, :c])
    i = jax.nn.sigmoid(jnp.concatenate([p[:, width:] for p in parts], axis=1) + gb_ref[:, c:])
    log_a = -LRU_C * r * sp_ref[...]
    a = jnp.exp(log_a)
    a_ref[...] = a
    b_ref[...] = jnp.sqrt(-(a * a + 1.0) * jnp.tanh(log_a)) * i * xc


def _lru_fwd_kernel(n_lat, n_ctx, xr_ref, prev_ref, next_ref, cw_ref, cb_ref, wg_ref, gb_ref, sp_ref,
                    xc_ref, hf_ref, a_ref, b_ref, h_ref):
    g = pl.program_id(0)

    @pl.when(g == 0)
    def _():
        h_ref[...] = jnp.zeros_like(h_ref)

    first = jnp.logical_or(g == 0, g == n_ctx)
    last = jnp.logical_or(g == n_ctx - 1, g == n_ctx + n_lat - 1)
    prev = jnp.where(first, 0.0, prev_ref[...])
    nxt = jnp.where(last, 0.0, next_ref[...])
    ext = jnp.concatenate([prev, xr_ref[...], nxt], axis=0)
    cw = cw_ref[...]
    xc = cb_ref[...] + cw[0:1] * ext[0:TM]
    for j in range(1, CONV_W):
        xc = xc + cw[j:j + 1] * ext[j * NB:j * NB + TM]
    xc_ref[...] = xc
    _rglru_coeffs(xc, wg_ref, gb_ref, sp_ref, a_ref, b_ref)
    h = h_ref[...]
    for t in range(TT):
        rows = pl.ds(t * NB, NB)
        h = a_ref[rows, :] * h + b_ref[rows, :]
        hf_ref[rows, :] = h
    h_ref[...] = h


def _lru_fwd(xr, conv_w, conv_b, wg, gb, sp, n_lat, n_ctx):
    rows, c = xr.shape
    halo_p, halo_n = 2 * NB, NB
    per_p, per_n = TM // halo_p, TM // halo_n
    chunk = lambda g: jnp.where(g < n_ctx, n_lat + g, g - n_ctx)
    const = lambda g: (0, 0)
    return pl.pallas_call(
        functools.partial(_lru_fwd_kernel, n_lat, n_ctx),
        grid=(n_lat + n_ctx,),
        in_specs=[pl.BlockSpec((TM, c), lambda g: (chunk(g), 0)),
                  pl.BlockSpec((halo_p, c), lambda g: (jnp.maximum(chunk(g) * per_p - 1, 0), 0)),
                  pl.BlockSpec((halo_n, c), lambda g: (jnp.minimum((chunk(g) + 1) * per_n, rows // halo_n - 1), 0)),
                  pl.BlockSpec((CONV_W, c), const),
                  pl.BlockSpec((1, c), const),
                  pl.BlockSpec(wg.shape, lambda g: (0, 0, 0)),
                  pl.BlockSpec((1, 2 * c), const),
                  pl.BlockSpec((1, c), const)],
        out_specs=[pl.BlockSpec((TM, c), lambda g: (chunk(g), 0)),
                   pl.BlockSpec((TM, c), lambda g: (chunk(g), 0))],
        out_shape=[jax.ShapeDtypeStruct((rows, c), F32), jax.ShapeDtypeStruct((rows, c), F32)],
        scratch_shapes=[pltpu.VMEM((TM, c), F32), pltpu.VMEM((TM, c), F32), pltpu.VMEM((NB, c), F32)],
        compiler_params=_cparams(("arbitrary",)),
        name="lru_fwd",
    )(xr, xr, xr, conv_w, conv_b, wg, gb, sp)


def _lru_bwd_kernel(alpha, xc_ref, hf_ref, gate_ref, x_ref, g1_ref, wg_ref, gb_ref, sp_ref, wo_ref,
                    lg_ref, lb_ref, o_ref, a_ref, b_ref, y_ref, h_ref):
    g = pl.program_id(0)

    @pl.when(g == 0)
    def _():
        h_ref[...] = jnp.zeros_like(h_ref)

    _rglru_coeffs(xc_ref[...], wg_ref, gb_ref, sp_ref, a_ref, b_ref)
    h = h_ref[...]
    for t in range(TT - 1, -1, -1):
        rows = pl.ds(t * NB, NB)
        h = a_ref[rows, :] * h + b_ref[rows, :]
        y_ref[rows, :] = h
    h_ref[...] = h
    y = ((hf_ref[...] + y_ref[...]) * gate_ref[...]).astype(BF16)
    delta = jnp.dot(y, wo_ref[...], preferred_element_type=F32)
    o_ref[...] = _post_norm(x_ref[...], delta, g1_ref[...], lg_ref[...], lb_ref[...], alpha)


def _lru_bwd(xc, hf, gate, x, g1, wg, gb, sp, w_out, lg, lb, n_lat, n_ctx, alpha):
    rows, c = xc.shape
    d = x.shape[1]
    chunk = lambda g: jnp.where(g < n_ctx, n_lat + (n_ctx - 1 - g), n_lat - 1 - (g - n_ctx))
    tile = lambda g: (chunk(g), 0)
    const = lambda g: (0, 0)
    return pl.pallas_call(
        functools.partial(_lru_bwd_kernel, alpha),
        grid=(n_lat + n_ctx,),
        in_specs=[pl.BlockSpec((TM, c), tile),
                  pl.BlockSpec((TM, c), tile),
                  pl.BlockSpec((TM, c), tile),
                  pl.BlockSpec((TM, d), tile),
                  pl.BlockSpec((None, NB, d), lambda g: (jnp.where(g < n_ctx, 1, 0), 0, 0)),
                  pl.BlockSpec(wg.shape, lambda g: (0, 0, 0)),
                  pl.BlockSpec((1, 2 * c), const),
                  pl.BlockSpec((1, c), const),
                  pl.BlockSpec((c, d), const),
                  pl.BlockSpec((1, d), const),
                  pl.BlockSpec((1, d), const)],
        out_specs=pl.BlockSpec((TM, d), tile),
        out_shape=jax.ShapeDtypeStruct((rows, d), F32),
        scratch_shapes=[pltpu.VMEM((TM, c), F32), pltpu.VMEM((TM, c), F32), pltpu.VMEM((TM, c), F32),
                        pltpu.VMEM((NB, c), F32)],
        compiler_params=_cparams(("arbitrary",)),
        name="lru_bwd",
    )(xc, hf, gate, x, g1, wg, gb, sp, w_out, lg, lb)


def _softplus_kernel(x_ref, o_ref):
    x = x_ref[...]
    o_ref[...] = jax.nn.softplus(x)


def _gate_weights(gate_w, gate_b):
    nblk, blk, _ = gate_w.shape
    c = nblk * blk
    per = next(n for n in range(1, nblk + 1) if nblk % n == 0 and (n * blk) % LANES == 0)
    groups, width = nblk // per, per * blk
    eye = jnp.eye(per, dtype=gate_w.dtype)
    gw = gate_w.reshape(groups, per, blk, 2 * blk)
    wr = jnp.einsum('gnde,nm->gndme', gw[..., :blk], eye).reshape(groups, width, width)
    wi = jnp.einsum('gnde,nm->gndme', gw[..., blk:], eye).reshape(groups, width, width)
    w = jnp.concatenate([wr, wi], axis=2).astype(BF16)
    b = jnp.concatenate([gate_b[:, :blk].reshape(1, c), gate_b[:, blk:].reshape(1, c)], axis=1)
    return w, b


def _qk_prep_kernel(qkv_ref, cos_ref, sa_ref, sb_ref, qg_ref, kg_ref, q_ref, k_ref, v_ref):
    cos, sa, sb = cos_ref[...], sa_ref[...], sb_ref[...]
    nq = N_Q_HEADS * HEAD_DIM
    nk = N_KV_HEADS * HEAD_DIM

    def norm_rope(xh, gain):
        ms = jnp.mean(xh * xh, axis=-1, keepdims=True)
        xn = xh * lax.rsqrt(ms + RMS_EPS) * gain
        return xn * cos + pltpu.roll(xn, 96, axis=1) * sa + pltpu.roll(xn, 32, axis=1) * sb

    for h in range(N_Q_HEADS):
        sl = slice(h * HEAD_DIM, (h + 1) * HEAD_DIM)
        q_ref[:, sl] = norm_rope(qkv_ref[:, sl], qg_ref[...]).astype(BF16)
    for h in range(N_KV_HEADS):
        sl = slice(h * HEAD_DIM, (h + 1) * HEAD_DIM)
        k_ref[:, sl] = norm_rope(qkv_ref[:, nq + h * HEAD_DIM:nq + (h + 1) * HEAD_DIM], kg_ref[...]).astype(BF16)
    v_ref[...] = qkv_ref[:, nq + nk:].astype(BF16)


def _qk_prep(qkv, cos, sa, sb, qg, kg):
    nb, lt, w = qkv.shape
    nq = N_Q_HEADS * HEAD_DIM
    nk = N_KV_HEADS * HEAD_DIM
    tl = TQ
    const = lambda b, i: (0, 0)
    tab = pl.BlockSpec((tl, HEAD_DIM), lambda b, i: (i, 0))
    return pl.pallas_call(
        _qk_prep_kernel,
        grid=(nb, lt // tl),
        in_specs=[pl.BlockSpec((None, tl, w), lambda b, i: (b, i, 0)), tab, tab, tab,
                  pl.BlockSpec((1, HEAD_DIM), const),
                  pl.BlockSpec((1, HEAD_DIM), const)],
        out_specs=[pl.BlockSpec((None, tl, nq), lambda b, i: (b, i, 0)),
                   pl.BlockSpec((None, tl, nk), lambda b, i: (b, i, 0)),
                   pl.BlockSpec((None, tl, nk), lambda b, i: (b, i, 0))],
        out_shape=[jax.ShapeDtypeStruct((nb, lt, nq), BF16),
                   jax.ShapeDtypeStruct((nb, lt, nk), BF16),
                   jax.ShapeDtypeStruct((nb, lt, nk), BF16)],
        compiler_params=_cparams(("arbitrary", "arbitrary")),
        name="qk_prep",
    )(qkv, cos, sa, sb, qg, kg)


def _attn_kernel(n_lat_tiles, seq, q_ref, k_ref, v_ref, o_ref):
    qi = pl.program_id(2)
    tq = q_ref.shape[0]
    scale = HEAD_DIM ** -0.5

    def attend(k, v):
        for j in range(Q_PER_KV):
            cols = slice(j * HEAD_DIM, (j + 1) * HEAD_DIM)
            s = lax.dot_general(q_ref[:, cols], k, (((1,), (1,)), ((), ())), preferred_element_type=F32) * scale
            m = jnp.max(s, axis=-1, keepdims=True)
            p = jnp.exp(s - m)
            l = jnp.sum(p, axis=-1, keepdims=True)
            o = jnp.dot(p.astype(BF16), v, preferred_element_type=F32) / l
            o_ref[:, cols] = o.astype(BF16)

    @pl.when(qi < n_lat_tiles)
    def _():
        attend(k_ref[...], v_ref[...])

    @pl.when(qi >= n_lat_tiles)
    def _():
        attend(k_ref[seq:, :], v_ref[seq:, :])


def _attention(q, k, v, seq, need_ctx):
    nb, lt, nq = q.shape
    tq = TQ
    gw = Q_PER_KV * HEAD_DIM
    return pl.pallas_call(
        functools.partial(_attn_kernel, seq // tq, seq),
        grid=(nb, N_KV_HEADS, (lt if need_ctx else seq) // tq),
        in_specs=[pl.BlockSpec((None, tq, gw), lambda b, h, i: (b, i, h)),
                  pl.BlockSpec((None, lt, HEAD_DIM), lambda b, h, i: (b, 0, h)),
                  pl.BlockSpec((None, lt, HEAD_DIM), lambda b, h, i: (b, 0, h))],
        out_specs=pl.BlockSpec((None, tq, gw), lambda b, h, i: (b, i, h)),
        out_shape=jax.ShapeDtypeStruct((nb, lt, nq), BF16),
        compiler_params=_cparams(("arbitrary", "arbitrary", "arbitrary")),
        name="attention",
    )(q, k, v)


def _attn_out_kernel(alpha, o_ref, x_ref, g1_ref, perm_ref, w_ref, lg_ref, lb_ref, out_ref):
    o_bt = o_ref[...].reshape(TM, o_ref.shape[-1])
    o_tb = jnp.dot(perm_ref[...], o_bt, preferred_element_type=F32).astype(BF16)
    delta = jnp.dot(o_tb, w_ref[...], preferred_element_type=F32)
    out_ref[...] = _post_norm(x_ref[...], delta, g1_ref[...], lg_ref[...], lb_ref[...], alpha)


def _attn_out(o, x, g1, w_out, lg, lb, n_lat_tiles, rows_out, alpha):
    d = x.shape[1]
    nq = w_out.shape[0]
    const = lambda i: (0, 0)
    return pl.pallas_call(
        functools.partial(_attn_out_kernel, alpha),
        grid=(rows_out // TM,),
        in_specs=[pl.BlockSpec((NB, TT, nq), lambda i: (0, i, 0)),
                  pl.BlockSpec((TM, d), lambda i: (i, 0)),
                  pl.BlockSpec((None, NB, d), lambda i: (i // n_lat_tiles, 0, 0)),
                  pl.BlockSpec((TM, TM), const),
                  pl.BlockSpec((nq, d), const),
                  pl.BlockSpec((1, d), const),
                  pl.BlockSpec((1, d), const)],
        out_specs=pl.BlockSpec((TM, d), lambda i: (i, 0)),
        out_shape=jax.ShapeDtypeStruct((rows_out, d), F32),
        compiler_params=_cparams(("arbitrary",)),
        name="attn_out",
    )(o, x, g1, _row_permutation().T, w_out, lg, lb)


def _rope_tables(seq, ctx_len):
    rows = seq // GRID_W
    row, col = jnp.meshgrid(jnp.arange(rows), jnp.arange(GRID_W), indexing='ij')
    pos = jnp.stack([row.reshape(-1), col.reshape(-1)], axis=-1).astype(F32)
    axis_dim = HEAD_DIM // 2
    inv_freq = ROPE_THETA ** (-jnp.arange(0, axis_dim, 2, dtype=F32) / axis_dim)
    ang = pos[:, :, None] * inv_freq
    cos, sin = jnp.cos(ang), jnp.sin(ang)
    zero = jnp.zeros_like(sin)
    cosf = jnp.concatenate([cos, cos], axis=-1).reshape(seq, HEAD_DIM)
    sa = jnp.concatenate([-sin, zero], axis=-1).reshape(seq, HEAD_DIM)
    sb = jnp.concatenate([zero, sin], axis=-1).reshape(seq, HEAD_DIM)
    pad = lambda t, v: jnp.concatenate([t, jnp.full((ctx_len, HEAD_DIM), v, F32)], axis=0)
    return pad(cosf, 1.0), pad(sa, 0.0), pad(sb, 0.0)


def _top16_rows(s, code, n, val_ref, idx_ref, cols):
    for r in range(PEER_TOPK):
        m = jnp.max(s, axis=0, keepdims=True)
        idx = jnp.min(jnp.where(s == m, code, float(n)), axis=0, keepdims=True)
        s = jnp.where(code == idx, -jnp.inf, s)
        val_ref[r:r + 1, cols] = m
        idx_ref[r:r + 1, cols] = idx


def _top16_keys(s, val_ref, idx_ref, cols):
    half = N_KEYS // 2
    a, b = s[:half], s[half:]
    row = lax.broadcasted_iota(I32, a.shape, 0).astype(F32)
    swap = b > a
    hi, lo = jnp.maximum(a, b), jnp.minimum(a, b)
    hi_idx = jnp.where(swap, row + half, row)
    lo_idx = jnp.where(swap, row, row + half)
    for r in range(PEER_TOPK):
        m = jnp.max(hi, axis=0, keepdims=True)
        idx = jnp.min(jnp.where(hi == m, hi_idx, float(N_KEYS)), axis=0, keepdims=True)
        taken = hi_idx == idx
        hi = jnp.where(taken, lo, hi)
        hi_idx = jnp.where(taken, lo_idx, hi_idx)
        lo = jnp.where(taken, -jnp.inf, lo)
        val_ref[r:r + 1, cols] = m
        idx_ref[r:r + 1, cols] = idx


_CAND_WIDE = [(a, PEER_TOPK // (a + 1)) for a in range(PEER_TOPK // 2)]
_CAND_TAIL = sum(n for _, n in _CAND_WIDE)
_CAND_USED = _CAND_TAIL + PEER_TOPK // 2
N_CAND = -(-_CAND_USED // 8) * 8


def _cand_codes():
    codes = [a * PEER_TOPK + b for a, n in _CAND_WIDE for b in range(n)]
    codes += [a * PEER_TOPK for a in range(PEER_TOPK // 2, PEER_TOPK)]
    codes += [PEER_TOPK * PEER_TOPK + i for i in range(N_CAND - len(codes))]
    return codes


def _peer_route_kernel(x_ref, sc_ref, sh_ref, wq_ref, sk_ref, code_ref, hb_ref, i1_ref, i2_ref, g_ref,
                       v1_ref, j1_ref, v2_ref, j2_ref, ts_ref, tp_ref, cand_ref, o1_ref, o2_ref, og_ref):
    h = _modulate(x_ref[...], sc_ref[...], sh_ref[...]).astype(BF16)
    hb_ref[...] = h
    hq = jnp.dot(h, wq_ref[...], preferred_element_type=F32).astype(BF16)
    tm = h.shape[0]
    nt = (((1,), (1,)), ((), ()))
    half = PEER_TOPK // 2
    cand_ref[_CAND_USED:, :] = jnp.full((N_CAND - _CAND_USED, tm), -jnp.inf, F32)
    for hd in range(PEER_HEADS):
        for p, (v_ref, j_ref) in enumerate(((v1_ref, j1_ref), (v2_ref, j2_ref))):
            qs = hq[:, (hd * 2 + p) * LANES:(hd * 2 + p + 1) * LANES]
            s = lax.dot_general(sk_ref[p], qs, nt, preferred_element_type=F32)
            for lb in range(tm // LANES):
                cols = slice(lb * LANES, (lb + 1) * LANES)
                _top16_keys(s[:, cols], v_ref, j_ref, cols)
        off = 0
        for a, n in _CAND_WIDE:
            cand_ref[off:off + n, :] = v1_ref[a:a + 1, :] + v2_ref[0:n, :]
            off += n
        cand_ref[_CAND_TAIL:_CAND_USED, :] = v1_ref[half:, :] + v2_ref[0:1, :]
        for lb in range(tm // LANES):
            cols = slice(lb * LANES, (lb + 1) * LANES)
            _top16_rows(cand_ref[:, cols], code_ref[:, cols], 2 * PEER_TOPK * PEER_TOPK, ts_ref, tp_ref, cols)
        ts = ts_ref[...]
        pos = tp_ref[...]
        ak = jnp.floor(pos * (1.0 / PEER_TOPK))
        bk = pos - PEER_TOPK * ak
        i1 = jnp.zeros_like(ts)
        i2 = jnp.zeros_like(ts)
        for a in range(PEER_TOPK):
            i1 = jnp.where(ak == a, j1_ref[a:a + 1, :], i1)
            i2 = jnp.where(bk == a, j2_ref[a:a + 1, :], i2)
        e = jnp.exp(ts - ts[0:1, :])
        rows = slice(hd * PEER_TOPK, (hd + 1) * PEER_TOPK)
        o1_ref[rows, :] = i1
        o2_ref[rows, :] = i2
        og_ref[rows, :] = e / jnp.sum(e, axis=0, keepdims=True)
    i1_ref[...] = o1_ref[...].T.astype(I32)
    i2_ref[...] = o2_ref[...].T.astype(I32)
    g_ref[...] = og_ref[...].T


def _peer_route(x, sc, sh, w_q, sub_keys, n_lat_tiles):
    rows, d = x.shape
    nq = w_q.shape[1]
    sel = lambda i: (i // n_lat_tiles, 0, 0)
    tile = lambda i: (i, 0)
    codes = jnp.broadcast_to(jnp.asarray(_cand_codes(), F32)[:, None], (N_CAND, TM))
    return pl.pallas_call(
        _peer_route_kernel,
        grid=(rows // TM,),
        in_specs=[pl.BlockSpec((TM, d), tile),
                  pl.BlockSpec((None, NB, d), sel),
                  pl.BlockSpec((None, NB, d), sel),
                  pl.BlockSpec((d, nq), lambda i: (0, 0)),
                  pl.BlockSpec(sub_keys.shape, lambda i: (0, 0, 0)),
                  pl.BlockSpec((N_CAND, TM), lambda i: (0, 0))],
        out_specs=[pl.BlockSpec((TM, d), tile),
                   pl.BlockSpec((TM, N_SLOTS), tile),
                   pl.BlockSpec((TM, N_SLOTS), tile),
                   pl.BlockSpec((TM, N_SLOTS), tile)],
        out_shape=[jax.ShapeDtypeStruct((rows, d), BF16),
                   jax.ShapeDtypeStruct((rows, N_SLOTS), I32),
                   jax.ShapeDtypeStruct((rows, N_SLOTS), I32),
                   jax.ShapeDtypeStruct((rows, N_SLOTS), F32)],
        scratch_shapes=[pltpu.VMEM((PEER_TOPK, TM), F32)] * 6
                       + [pltpu.VMEM((N_CAND, TM), F32)]
                       + [pltpu.VMEM((N_SLOTS, TM), F32)] * 3,
        compiler_params=_cparams(("arbitrary",)),
        name="peer_route",
    )(x, sc, sh, w_q, sub_keys, codes)


def _peer_act_kernel(n_steps, hb_ref, u_ref, i1_ref, i2_ref, g_ref, w_ref, z_ref, sa_ref, sb_ref):
    cj = pl.program_id(1)
    per_step = EXP_CHUNK // N_KEYS

    def scores(dst_ref):
        dst_ref[...] = lax.dot_general(hb_ref[...], u_ref[...], (((1,), (1,)), ((), ())),
                                       preferred_element_type=F32)

    def gather(src_ref):
        i1 = i1_ref[...]
        i2 = i2_ref[...]
        z = z_ref[...]
        for q in range(per_step):
            picked = jnp.take_along_axis(src_ref[:, q * N_KEYS:(q + 1) * N_KEYS], i2, axis=1)
            z = jnp.where(i1 == (cj - 1) * per_step + q, picked, z)
        z_ref[...] = z
        return z

    @pl.when(cj == 0)
    def _():
        z_ref[...] = jnp.zeros_like(z_ref)
        scores(sa_ref)

    steady = jnp.logical_and(cj > 0, cj < n_steps)

    @pl.when(jnp.logical_and(steady, cj % 2 == 1))
    def _():
        gather(sa_ref)
        scores(sb_ref)

    @pl.when(jnp.logical_and(steady, cj % 2 == 0))
    def _():
        gather(sb_ref)
        scores(sa_ref)

    @pl.when(cj == n_steps)
    def _():
        z = gather(sa_ref if n_steps % 2 == 1 else sb_ref)
        w_ref[...] = g_ref[...] * _gelu(z)


def _peer_act(hb, u_tab, i1, i2, g):
    rows, d = hb.shape
    n_exp = u_tab.shape[0]
    n_steps = n_exp // EXP_CHUNK
    tile = lambda i, j: (i, 0)
    return pl.pallas_call(
        functools.partial(_peer_act_kernel, n_steps),
        grid=(rows // TM_ACT, n_steps + 1),
        in_specs=[pl.BlockSpec((TM_ACT, d), tile),
                  pl.BlockSpec((EXP_CHUNK, d), lambda i, j: (jnp.minimum(j, n_steps - 1), 0)),
                  pl.BlockSpec((TM_ACT, N_SLOTS), tile),
                  pl.BlockSpec((TM_ACT, N_SLOTS), tile),
                  pl.BlockSpec((TM_ACT, N_SLOTS), tile)],
        out_specs=pl.BlockSpec((TM_ACT, N_SLOTS), tile),
        out_shape=jax.ShapeDtypeStruct((rows, N_SLOTS), F32),
        scratch_shapes=[pltpu.VMEM((TM_ACT, N_SLOTS), F32),
                        pltpu.VMEM((TM_ACT, EXP_CHUNK), F32),
                        pltpu.VMEM((TM_ACT, EXP_CHUNK), F32)],
        compiler_params=_cparams(("arbitrary", "arbitrary")),
        name="peer_act",
    )(hb, u_tab, i1, i2, g)


def _peer_out_kernel(alpha, i1_ref, i2_ref, w_ref, v_ref, x_ref, g2_ref, lg_ref, lb_ref, o_ref,
                     wbuf_ref, wd_ref, acc_ref):
    vj = pl.program_id(1)
    tm = x_ref.shape[0]
    groups = VAL_CHUNK // N_KEYS

    @pl.when(vj == 0)
    def _():
        acc_ref[...] = jnp.zeros_like(acc_ref)
        key_iota = lax.broadcasted_iota(I32, (N_KEYS, N_SLOTS), 0)
        nt = (((1,), (1,)), ((), ()))

        for sub in range(tm // ASM_TM):
            base = sub * ASM_TM

            def one_hots(t):
                row = pl.ds(base + t, 1)
                i1 = jnp.broadcast_to(i1_ref[row, :], (N_KEYS, N_SLOTS))
                i2 = jnp.broadcast_to(i2_ref[row, :], (N_KEYS, N_SLOTS))
                w = jnp.broadcast_to(w_ref[row, :], (N_KEYS, N_SLOTS))
                return (jnp.where(i1 == key_iota, w, 0.0).astype(BF16),
                        jnp.where(i2 == key_iota, 1.0, 0.0).astype(BF16))

            def assemble(tb, carry):
                for u in range(ASM_UNROLL):
                    t = tb * ASM_UNROLL + u
                    lhs, rhs = one_hots(t)
                    wbuf_ref[pl.ds(pl.multiple_of(t * W_STRIDE, 8), N_KEYS), :] = lax.dot_general(
                        lhs, rhs, nt, preferred_element_type=F32)
                return carry

            lax.fori_loop(0, ASM_TM // ASM_UNROLL, assemble, 0)

            def regroup(cb, carry):
                for u in range(ASM_UNROLL):
                    c = cb * ASM_UNROLL + u
                    wd_ref[c, base:base + ASM_TM, :] = wbuf_ref[pl.ds(c, ASM_TM, stride=W_STRIDE), :].astype(BF16)
                return carry

            lax.fori_loop(0, N_KEYS // ASM_UNROLL, regroup, 0)

    lhs = jnp.concatenate([wd_ref[vj * groups + q] for q in range(groups)], axis=1)
    acc_ref[...] += jnp.dot(lhs, v_ref[...], preferred_element_type=F32)

    @pl.when(vj == pl.num_programs(1) - 1)
    def _():
        o_ref[...] = _post_norm(x_ref[...], acc_ref[...], g2_ref[...], lg_ref[...], lb_ref[...], alpha)


def _peer_out(i1, i2, w, v_tab, x, g2, lg, lb, n_lat_tiles, alpha):
    rows, d = x.shape
    n_exp = v_tab.shape[0]
    tile = lambda i, j: (i, 0)
    const = lambda i, j: (0, 0)
    return pl.pallas_call(
        functools.partial(_peer_out_kernel, alpha),
        grid=(rows // TM_OUT, n_exp // VAL_CHUNK),
        in_specs=[pl.BlockSpec((TM_OUT, N_SLOTS), tile),
                  pl.BlockSpec((TM_OUT, N_SLOTS), tile),
                  pl.BlockSpec((TM_OUT, N_SLOTS), tile),
                  pl.BlockSpec((VAL_CHUNK, d), lambda i, j: (j, 0)),
                  pl.BlockSpec((TM_OUT, d), tile),
                  pl.BlockSpec((None, NB, d), lambda i, j: (i // n_lat_tiles, 0, 0)),
                  pl.BlockSpec((1, d), const),
                  pl.BlockSpec((1, d), const)],
        out_specs=pl.BlockSpec((TM_OUT, d), tile),
        out_shape=jax.ShapeDtypeStruct((rows, d), F32),
        scratch_shapes=[pltpu.VMEM((ASM_TM * W_STRIDE, N_KEYS), F32),
                        pltpu.VMEM((n_exp // N_KEYS, TM_OUT, N_KEYS), BF16),
                        pltpu.VMEM((TM_OUT, d), F32)],
        compiler_params=_cparams(("arbitrary", "arbitrary")),
        name="peer_out",
    )(i1, i2, w, v_tab, x, g2, lg, lb)


def kernel(x, c, ctx, c_ctx, mod_w, mod_b, ln_g, ln_b, lru_w_in, lru_conv_w, lru_conv_b, lru_gate_w, lru_gate_b,
           lru_lambda, lru_w_out, attn_w_qkv, attn_q_gain, attn_k_gain, attn_w_out, peer_w_q, peer_sub_keys,
           peer_u, peer_v):
    nb, seq, d = x.shape
    ctx_len = ctx.shape[1]
    depth = mod_w.shape[0]
    assert nb == NB and seq % GRID_W == 0 and seq % TQ == 0 and ctx_len % TQ == 0 and (seq + ctx_len) % TL_PREP == 0
    for t in (TM, TM_ACT, TM_OUT):
        assert (seq * NB) % t == 0 and (ctx_len * NB) % t == 0
    lt = seq + ctx_len
    n_lat, n_ctx = seq * NB // TM, ctx_len * NB // TM
    alpha = (2 * depth) ** 0.25

    xs = jnp.concatenate([x.transpose(1, 0, 2), ctx.transpose(1, 0, 2)], axis=0).reshape(lt * NB, d)

    cond = jnp.concatenate([c, c_ctx[None], jnp.zeros((7, d), F32)], axis=0)
    mtab = _modulation_table(cond, mod_w, mod_b)
    lat = mtab[:, :NB].reshape(depth, NB, 6, d).transpose(0, 2, 1, 3)
    cx = jnp.broadcast_to(mtab[:, NB].reshape(depth, 6, 1, d), (depth, 6, NB, d))
    mods = jnp.stack([lat, cx], axis=2)

    cos, sa, sb = _rope_tables(seq, ctx_len)
    lam = lru_lambda.reshape(-1, lru_lambda.shape[-1])
    softplus_neg_lam = pl.pallas_call(
        _softplus_kernel, out_shape=jax.ShapeDtypeStruct(lam.shape, F32), name="softplus")(-lam)
    softplus_neg_lam = softplus_neg_lam.reshape(lru_lambda.shape)

    for i in range(depth):
        sh1, sc1, g1, sh2, sc2, g2 = (mods[i, j] for j in range(6))
        lg1, lb1 = ln_g[i, 0][None], ln_b[i, 0][None]
        lg2, lb2 = ln_g[i, 1][None], ln_b[i, 1][None]
        j = i // 2
        if i % 2 == 0:
            gate, xr = _lru_in(xs, sc1, sh1, lru_w_in[j].astype(BF16), n_lat)
            wg0, gb0 = _gate_weights(lru_gate_w[j, 0], lru_gate_b[j, 0])
            wg1, gb1 = _gate_weights(lru_gate_w[j, 1], lru_gate_b[j, 1])
            xc, hf = _lru_fwd(xr, lru_conv_w[j], lru_conv_b[j][None], wg0, gb0, softplus_neg_lam[j, 0][None],
                              n_lat, n_ctx)
            xs = _lru_bwd(xc, hf, gate, xs, g1, wg1, gb1, softplus_neg_lam[j, 1][None],
                          lru_w_out[j].astype(BF16), lg1, lb1, n_lat, n_ctx, alpha)
        else:
            qkv = _qkv(xs, sc1, sh1, attn_w_qkv[j].astype(BF16), n_lat)
            q, k, v = _qk_prep(qkv, cos, sa, sb, attn_q_gain[j][None], attn_k_gain[j][None])
            need_ctx = i < depth - 1
            o = _attention(q, k, v, seq, need_ctx)
            xs = _attn_out(o, xs, g1, attn_w_out[j].astype(BF16), lg1, lb1, n_lat,
                           (lt if need_ctx else seq) * NB, alpha)
        hb, i1, i2, g = _peer_route(xs, sc2, sh2, peer_w_q[i].astype(BF16), peer_sub_keys[i].astype(BF16), n_lat)
        w = _peer_act(hb, peer_u[i].astype(BF16), i1, i2, g)
        xs = _peer_out(i1, i2, w, peer_v[i].astype(BF16), xs, g2, lg2, lb2, seq * NB // TM_OUT, alpha)

    return xs.reshape(-1, NB, d)[:seq].transpose(1, 0, 2)
```

```python
import functools

import jax
import jax.numpy as jnp
from jax import lax
from jax.experimental import pallas as pl
from jax.experimental.pallas import tpu as pltpu

F32 = jnp.float32
BF16 = jnp.bfloat16
I32 = jnp.int32

NB = 16
GRID_W = 64
CONV_W = 4
LRU_C = 8.0
N_Q_HEADS = 8
N_KV_HEADS = 2
HEAD_DIM = 128
Q_PER_KV = N_Q_HEADS // N_KV_HEADS
ROPE_THETA = 10000.0
PEER_HEADS = 8
N_KEYS = 128
PEER_TOPK = 16
N_SLOTS = PEER_HEADS * PEER_TOPK
LN_EPS = 1e-6
RMS_EPS = 1e-6
LANES = 128
VMEM_LIMIT = 56 * 1024 * 1024

TM = 256
TT = TM // NB
TQ = 256
TM_ACT = 1024
EXP_CHUNK = 1024
VAL_CHUNK = 4096
W_STRIDE = N_KEYS + 8
ASM_UNROLL = 32
TM_OUT = 512
ASM_TM = 64


def _cparams(sem):
    return pltpu.CompilerParams(dimension_semantics=sem, vmem_limit_bytes=VMEM_LIMIT)


def _gelu(x):
    return 0.5 * x * (1.0 + lax.erf(x * 0.7071067811865476))


def _modulate(x, sc, sh):
    rows, d = x.shape
    x3 = x.reshape(rows // NB, NB, d)
    return (x3 * (1.0 + sc)[None] + sh[None]).reshape(rows, d)


def _post_norm(x, delta, gate, lg, lb, alpha):
    rows, d = x.shape
    v = alpha * x + (delta.reshape(rows // NB, NB, d) * gate[None]).reshape(rows, d)
    mu = jnp.mean(v, axis=-1, keepdims=True)
    vc = v - mu
    var = jnp.mean(vc * vc, axis=-1, keepdims=True)
    return vc * lax.rsqrt(var + LN_EPS) * lg + lb


def _mod_kernel(c_ref, w_ref, b_ref, o_ref):
    c = c_ref[...]
    s = (c * jax.nn.sigmoid(c)).astype(BF16)
    o_ref[...] = jnp.dot(s, w_ref[...].astype(BF16), preferred_element_type=F32) + b_ref[...]


def _modulation_table(cond, mod_w, mod_b):
    depth, d, d6 = mod_w.shape
    nblk = d6 // d
    return pl.pallas_call(
        _mod_kernel,
        grid=(depth, nblk),
        in_specs=[pl.BlockSpec((cond.shape[0], d), lambda l, j: (0, 0)),
                  pl.BlockSpec((None, d, d), lambda l, j: (l, 0, j)),
                  pl.BlockSpec((None, 1, d), lambda l, j: (l, 0, j))],
        out_specs=pl.BlockSpec((None, cond.shape[0], d), lambda l, j: (l, 0, j)),
        out_shape=jax.ShapeDtypeStruct((depth, cond.shape[0], d6), F32),
        compiler_params=_cparams(("arbitrary", "arbitrary")),
        name="modulation",
    )(cond, mod_w, mod_b.reshape(depth, 1, d6))


def _lru_in_kernel(x_ref, sc_ref, sh_ref, w_ref, gate_ref, xr_ref):
    h = _modulate(x_ref[...], sc_ref[...], sh_ref[...]).astype(BF16)
    u = jnp.dot(h, w_ref[...], preferred_element_type=F32)
    c = gate_ref.shape[-1]
    gate_ref[...] = _gelu(u[:, :c])
    xr_ref[...] = u[:, c:]


def _lru_in(x, sc, sh, w_in, n_lat_tiles):
    rows, d = x.shape
    c2 = w_in.shape[1]
    c = c2 // 2
    sel = lambda i: (i // n_lat_tiles, 0, 0)
    return pl.pallas_call(
        _lru_in_kernel,
        grid=(rows // TM,),
        in_specs=[pl.BlockSpec((TM, d), lambda i: (i, 0)),
                  pl.BlockSpec((None, NB, d), sel),
                  pl.BlockSpec((None, NB, d), sel),
                  pl.BlockSpec((d, c2), lambda i: (0, 0))],
        out_specs=[pl.BlockSpec((TM, c), lambda i: (i, 0)),
                   pl.BlockSpec((TM, c), lambda i: (i, 0))],
        out_shape=[jax.ShapeDtypeStruct((rows, c), F32), jax.ShapeDtypeStruct((rows, c), F32)],
        compiler_params=_cparams(("arbitrary",)),
        name="lru_in",
    )(x, sc, sh, w_in)


def _row_permutation():
    r = jnp.arange(TM)
    src = (r % TT) * NB + r // TT
    return (src[:, None] == jnp.arange(TM)[None, :]).astype(BF16)


def _qkv_kernel(x_ref, sc_ref, sh_ref, perm_ref, w_ref, cos_ref, sa_ref, sb_ref, qg_ref, kg_ref,
                q_ref, k_ref, v_ref):
    h = _modulate(x_ref[...], sc_ref[...], sh_ref[...]).astype(BF16)
    hp = jnp.dot(perm_ref[...], h, preferred_element_type=F32).astype(BF16)
    qkv = jnp.dot(hp, w_ref[...], preferred_element_type=F32)
    nq = N_Q_HEADS * HEAD_DIM
    nk = N_KV_HEADS * HEAD_DIM
    per_row = lambda t_ref: jnp.broadcast_to(t_ref[...][None], (NB, TT, HEAD_DIM)).reshape(TM, HEAD_DIM)
    cos, sa, sb = per_row(cos_ref), per_row(sa_ref), per_row(sb_ref)

    def norm_rope(xh, gain):
        ms = jnp.mean(xh * xh, axis=-1, keepdims=True)
        xn = xh * lax.rsqrt(ms + RMS_EPS) * gain
        return xn * cos + pltpu.roll(xn, 96, axis=1) * sa + pltpu.roll(xn, 32, axis=1) * sb

    q = [norm_rope(qkv[:, h * HEAD_DIM:(h + 1) * HEAD_DIM], qg_ref[...]) for h in range(N_Q_HEADS)]
    k = [norm_rope(qkv[:, nq + h * HEAD_DIM:nq + (h + 1) * HEAD_DIM], kg_ref[...]) for h in range(N_KV_HEADS)]
    q_ref[...] = jnp.concatenate(q, axis=1).astype(BF16).reshape(NB, TT, nq)
    k_ref[...] = jnp.concatenate(k, axis=1).astype(BF16).reshape(NB, TT, nk)
    v_ref[...] = qkv[:, nq + nk:].astype(BF16).reshape(NB, TT, nk)


def _qkv(x, sc, sh, w, cos, sa, sb, qg, kg, n_lat_tiles):
    rows, d = x.shape
    lt = rows // NB
    nq = N_Q_HEADS * HEAD_DIM
    nk = N_KV_HEADS * HEAD_DIM
    sel = lambda i: (i // n_lat_tiles, 0, 0)
    const = lambda i: (0, 0)
    tab = pl.BlockSpec((TT, HEAD_DIM), lambda i: (i, 0))
    return pl.pallas_call(
        _qkv_kernel,
        grid=(rows // TM,),
        in_specs=[pl.BlockSpec((TM, d), lambda i: (i, 0)),
                  pl.BlockSpec((None, NB, d), sel),
                  pl.BlockSpec((None, NB, d), sel),
                  pl.BlockSpec((TM, TM), const),
                  pl.BlockSpec((d, nq + 2 * nk), const),
                  tab, tab, tab,
                  pl.BlockSpec((1, HEAD_DIM), const),
                  pl.BlockSpec((1, HEAD_DIM), const)],
        out_specs=[pl.BlockSpec((NB, TT, nq), lambda i: (0, i, 0)),
                   pl.BlockSpec((NB, TT, nk), lambda i: (0, i, 0)),
                   pl.BlockSpec((NB, TT, nk), lambda i: (0, i, 0))],
        out_shape=[jax.ShapeDtypeStruct((NB, lt, nq), BF16),
                   jax.ShapeDtypeStruct((NB, lt, nk), BF16),
                   jax.ShapeDtypeStruct((NB, lt, nk), BF16)],
        compiler_params=_cparams(("arbitrary",)),
        name="qkv",
    )(x, sc, sh, _row_permutation(), w, cos, sa, sb, qg, kg)


def _rglru_coeffs(xc, wg_ref, gb_ref, sp_ref, a_ref, b_ref):
    c = xc.shape[-1]
    groups, width, _ = wg_ref.shape
    xb = xc.astype(BF16)
    parts = [jnp.dot(xb[:, s * width:(s + 1) * width], wg_ref[s], preferred_element_type=F32)
             for s in range(groups)]
    r = jax.nn.sigmoid(jnp.concatenate([p[:, :width] for p in parts], axis=1) + gb_ref[:, :c])
    i = jax.nn.sigmoid(jnp.concatenate([p[:, width:] for p in parts], axis=1) + gb_ref[:, c:])
    log_a = -LRU_C * r * sp_ref[...]
    a = jnp.exp(log_a)
    a_ref[...] = a
    b_ref[...] = jnp.sqrt(-(a * a + 1.0) * jnp.tanh(log_a)) * i * xc


def _lru_fwd_kernel(n_lat, n_ctx, xr_ref, prev_ref, next_ref, cw_ref, cb_ref, wg_ref, gb_ref, sp_ref,
                    xc_ref, hf_ref, a_ref, b_ref, h_ref):
    g = pl.program_id(0)

    @pl.when(g == 0)
    def _():
        h_ref[...] = jnp.zeros_like(h_ref)

    first = jnp.logical_or(g == 0, g == n_ctx)
    last = jnp.logical_or(g == n_ctx - 1, g == n_ctx + n_lat - 1)
    prev = jnp.where(first, 0.0, prev_ref[...])
    nxt = jnp.where(last, 0.0, next_ref[...])
    ext = jnp.concatenate([prev, xr_ref[...], nxt], axis=0)
    cw = cw_ref[...]
    xc = cb_ref[...] + cw[0:1] * ext[0:TM]
    for j in range(1, CONV_W):
        xc = xc + cw[j:j + 1] * ext[j * NB:j * NB + TM]
    xc_ref[...] = xc
    _rglru_coeffs(xc, wg_ref, gb_ref, sp_ref, a_ref, b_ref)
    h = h_ref[...]
    for t in range(TT):
        rows = pl.ds(t * NB, NB)
        h = a_ref[rows, :] * h + b_ref[rows, :]
        hf_ref[rows, :] = h
    h_ref[...] = h


def _lru_fwd(xr, conv_w, conv_b, wg, gb, sp, n_lat, n_ctx):
    rows, c = xr.shape
    halo_p, halo_n = 2 * NB, NB
    per_p, per_n = TM // halo_p, TM // halo_n
    chunk = lambda g: jnp.where(g < n_ctx, n_lat + g, g - n_ctx)
    const = lambda g: (0, 0)
    return pl.pallas_call(
        functools.partial(_lru_fwd_kernel, n_lat, n_ctx),
        grid=(n_lat + n_ctx,),
        in_specs=[pl.BlockSpec((TM, c), lambda g: (chunk(g), 0)),
                  pl.BlockSpec((halo_p, c), lambda g: (jnp.maximum(chunk(g) * per_p - 1, 0), 0)),
                  pl.BlockSpec((halo_n, c), lambda g: (jnp.minimum((chunk(g) + 1) * per_n, rows // halo_n - 1), 0)),
                  pl.BlockSpec((CONV_W, c), const),
                  pl.BlockSpec((1, c), const),
                  pl.BlockSpec(wg.shape, lambda g: (0, 0, 0)),
                  pl.BlockSpec((1, 2 * c), const),
                  pl.BlockSpec((1, c), const)],
        out_specs=[pl.BlockSpec((TM, c), lambda g: (chunk(g), 0)),
                   pl.BlockSpec((TM, c), lambda g: (chunk(g), 0))],
        out_shape=[jax.ShapeDtypeStruct((rows, c), F32), jax.ShapeDtypeStruct((rows, c), F32)],
        scratch_shapes=[pltpu.VMEM((TM, c), F32), pltpu.VMEM((TM, c), F32), pltpu.VMEM((NB, c), F32)],
        compiler_params=_cparams(("arbitrary",)),
        name="lru_fwd",
    )(xr, xr, xr, conv_w, conv_b, wg, gb, sp)


def _lru_bwd_kernel(alpha, xc_ref, hf_ref, gate_ref, x_ref, g1_ref, wg_ref, gb_ref, sp_ref, wo_ref,
                    lg_ref, lb_ref, o_ref, a_ref, b_ref, y_ref, h_ref):
    g = pl.program_id(0)

    @pl.when(g == 0)
    def _():
        h_ref[...] = jnp.zeros_like(h_ref)

    _rglru_coeffs(xc_ref[...], wg_ref, gb_ref, sp_ref, a_ref, b_ref)
    h = h_ref[...]
    for t in range(TT - 1, -1, -1):
        rows = pl.ds(t * NB, NB)
        h = a_ref[rows, :] * h + b_ref[rows, :]
        y_ref[rows, :] = h
    h_ref[...] = h
    y = ((hf_ref[...] + y_ref[...]) * gate_ref[...]).astype(BF16)
    delta = jnp.dot(y, wo_ref[...], preferred_element_type=F32)
    o_ref[...] = _post_norm(x_ref[...], delta, g1_ref[...], lg_ref[...], lb_ref[...], alpha)


def _lru_bwd(xc, hf, gate, x, g1, wg, gb, sp, w_out, lg, lb, n_lat, n_ctx, alpha):
    rows, c = xc.shape
    d = x.shape[1]
    chunk = lambda g: jnp.where(g < n_ctx, n_lat + (n_ctx - 1 - g), n_lat - 1 - (g - n_ctx))
    tile = lambda g: (chunk(g), 0)
    const = lambda g: (0, 0)
    return pl.pallas_call(
        functools.partial(_lru_bwd_kernel, alpha),
        grid=(n_lat + n_ctx,),
        in_specs=[pl.BlockSpec((TM, c), tile),
                  pl.BlockSpec((TM, c), tile),
                  pl.BlockSpec((TM, c), tile),
                  pl.BlockSpec((TM, d), tile),
                  pl.BlockSpec((None, NB, d), lambda g: (jnp.where(g < n_ctx, 1, 0), 0, 0)),
                  pl.BlockSpec(wg.shape, lambda g: (0, 0, 0)),
                  pl.BlockSpec((1, 2 * c), const),
                  pl.BlockSpec((1, c), const),
                  pl.BlockSpec((c, d), const),
                  pl.BlockSpec((1, d), const),
                  pl.BlockSpec((1, d), const)],
        out_specs=pl.BlockSpec((TM, d), tile),
        out_shape=jax.ShapeDtypeStruct((rows, d), F32),
        scratch_shapes=[pltpu.VMEM((TM, c), F32), pltpu.VMEM((TM, c), F32), pltpu.VMEM((TM, c), F32),
                        pltpu.VMEM((NB, c), F32)],
        compiler_params=_cparams(("arbitrary",)),
        name="lru_bwd",
    )(xc, hf, gate, x, g1, wg, gb, sp, w_out, lg, lb)


def _softplus_kernel(x_ref, o_ref):
    x = x_ref[...]
    o_ref[...] = jax.nn.softplus(x)


def _gate_weights(gate_w, gate_b):
    nblk, blk, _ = gate_w.shape
    c = nblk * blk
    per = next(n for n in range(1, nblk + 1) if nblk % n == 0 and (n * blk) % LANES == 0)
    groups, width = nblk // per, per * blk
    eye = jnp.eye(per, dtype=gate_w.dtype)
    gw = gate_w.reshape(groups, per, blk, 2 * blk)
    wr = jnp.einsum('gnde,nm->gndme', gw[..., :blk], eye).reshape(groups, width, width)
    wi = jnp.einsum('gnde,nm->gndme', gw[..., blk:], eye).reshape(groups, width, width)
    w = jnp.concatenate([wr, wi], axis=2).astype(BF16)
    b = jnp.concatenate([gate_b[:, :blk].reshape(1, c), gate_b[:, blk:].reshape(1, c)], axis=1)
    return w, b


def _attn_kernel(n_lat_tiles, seq, q_ref, k_ref, v_ref, o_ref):
    qi = pl.program_id(2)
    tq = q_ref.shape[0]
    scale = HEAD_DIM ** -0.5

    def attend(k, v):
        for j in range(Q_PER_KV):
            cols = slice(j * HEAD_DIM, (j + 1) * HEAD_DIM)
            s = lax.dot_general(q_ref[:, cols], k, (((1,), (1,)), ((), ())), preferred_element_type=F32) * scale
            m = jnp.max(s, axis=-1, keepdims=True)
            p = jnp.exp(s - m)
            l = jnp.sum(p, axis=-1, keepdims=True)
            o = jnp.dot(p.astype(BF16), v, preferred_element_type=F32) / l
            o_ref[:, cols] = o.astype(BF16)

    @pl.when(qi < n_lat_tiles)
    def _():
        attend(k_ref[...], v_ref[...])

    @pl.when(qi >= n_lat_tiles)
    def _():
        attend(k_ref[seq:, :], v_ref[seq:, :])


def _attention(q, k, v, seq, need_ctx):
    nb, lt, nq = q.shape
    tq = TQ
    gw = Q_PER_KV * HEAD_DIM
    return pl.pallas_call(
        functools.partial(_attn_kernel, seq // tq, seq),
        grid=(nb, N_KV_HEADS, (lt if need_ctx else seq) // tq),
        in_specs=[pl.BlockSpec((None, tq, gw), lambda b, h, i: (b, i, h)),
                  pl.BlockSpec((None, lt, HEAD_DIM), lambda b, h, i: (b, 0, h)),
                  pl.BlockSpec((None, lt, HEAD_DIM), lambda b, h, i: (b, 0, h))],
        out_specs=pl.BlockSpec((None, tq, gw), lambda b, h, i: (b, i, h)),
        out_shape=jax.ShapeDtypeStruct((nb, lt, nq), BF16),
        compiler_params=_cparams(("arbitrary", "arbitrary", "arbitrary")),
        name="attention",
    )(q, k, v)


def _attn_out_kernel(alpha, o_ref, x_ref, g1_ref, perm_ref, w_ref, lg_ref, lb_ref, out_ref):
    o_bt = o_ref[...].reshape(TM, o_ref.shape[-1])
    o_tb = jnp.dot(perm_ref[...], o_bt, preferred_element_type=F32).astype(BF16)
    delta = jnp.dot(o_tb, w_ref[...], preferred_element_type=F32)
    out_ref[...] = _post_norm(x_ref[...], delta, g1_ref[...], lg_ref[...], lb_ref[...], alpha)


def _attn_out(o, x, g1, w_out, lg, lb, n_lat_tiles, rows_out, alpha):
    d = x.shape[1]
    nq = w_out.shape[0]
    const = lambda i: (0, 0)
    return pl.pallas_call(
        functools.partial(_attn_out_kernel, alpha),
        grid=(rows_out // TM,),
        in_specs=[pl.BlockSpec((NB, TT, nq), lambda i: (0, i, 0)),
                  pl.BlockSpec((TM, d), lambda i: (i, 0)),
                  pl.BlockSpec((None, NB, d), lambda i: (i // n_lat_tiles, 0, 0)),
                  pl.BlockSpec((TM, TM), const),
                  pl.BlockSpec((nq, d), const),
                  pl.BlockSpec((1, d), const),
                  pl.BlockSpec((1, d), const)],
        out_specs=pl.BlockSpec((TM, d), lambda i: (i, 0)),
        out_shape=jax.ShapeDtypeStruct((rows_out, d), F32),
        compiler_params=_cparams(("arbitrary",)),
        name="attn_out",
    )(o, x, g1, _row_permutation().T, w_out, lg, lb)


def _rope_tables(seq, ctx_len):
    rows = seq // GRID_W
    row, col = jnp.meshgrid(jnp.arange(rows), jnp.arange(GRID_W), indexing='ij')
    pos = jnp.stack([row.reshape(-1), col.reshape(-1)], axis=-1).astype(F32)
    axis_dim = HEAD_DIM // 2
    inv_freq = ROPE_THETA ** (-jnp.arange(0, axis_dim, 2, dtype=F32) / axis_dim)
    ang = pos[:, :, None] * inv_freq
    cos, sin = jnp.cos(ang), jnp.sin(ang)
    zero = jnp.zeros_like(sin)
    cosf = jnp.concatenate([cos, cos], axis=-1).reshape(seq, HEAD_DIM)
    sa = jnp.concatenate([-sin, zero], axis=-1).reshape(seq, HEAD_DIM)
    sb = jnp.concatenate([zero, sin], axis=-1).reshape(seq, HEAD_DIM)
    pad = lambda t, v: jnp.concatenate([t, jnp.full((ctx_len, HEAD_DIM), v, F32)], axis=0)
    return pad(cosf, 1.0), pad(sa, 0.0), pad(sb, 0.0)


def _top16_rows(s, code, n, val_ref, idx_ref, cols):
    for r in range(PEER_TOPK):
        m = jnp.max(s, axis=0, keepdims=True)
        idx = jnp.min(jnp.where(s == m, code, float(n)), axis=0, keepdims=True)
        s = jnp.where(code == idx, -jnp.inf, s)
        val_ref[r:r + 1, cols] = m
        idx_ref[r:r + 1, cols] = idx


def _top16_keys(s, val_ref, idx_ref, cols):
    half = N_KEYS // 2
    a, b = s[:half], s[half:]
    row = lax.broadcasted_iota(I32, a.shape, 0).astype(F32)
    swap = b > a
    hi, lo = jnp.maximum(a, b), jnp.minimum(a, b)
    hi_idx = jnp.where(swap, row + half, row)
    lo_idx = jnp.where(swap, row, row + half)
    for r in range(PEER_TOPK):
        m = jnp.max(hi, axis=0, keepdims=True)
        idx = jnp.min(jnp.where(hi == m, hi_idx, float(N_KEYS)), axis=0, keepdims=True)
        taken = hi_idx == idx
        hi = jnp.where(taken, lo, hi)
        hi_idx = jnp.where(taken, lo_idx, hi_idx)
        lo = jnp.where(taken, -jnp.inf, lo)
        val_ref[r:r + 1, cols] = m
        idx_ref[r:r + 1, cols] = idx


_CAND_WIDE = [(a, PEER_TOPK // (a + 1)) for a in range(PEER_TOPK // 2)]
_CAND_TAIL = sum(n for _, n in _CAND_WIDE)
_CAND_USED = _CAND_TAIL + PEER_TOPK // 2
N_CAND = -(-_CAND_USED // 8) * 8


def _cand_codes():
    codes = [a * PEER_TOPK + b for a, n in _CAND_WIDE for b in range(n)]
    codes += [a * PEER_TOPK for a in range(PEER_TOPK // 2, PEER_TOPK)]
    codes += [PEER_TOPK * PEER_TOPK + i for i in range(N_CAND - len(codes))]
    return codes


def _peer_route_kernel(x_ref, sc_ref, sh_ref, wq_ref, sk_ref, code_ref, hb_ref, i1_ref, i2_ref, g_ref,
                       v1_ref, j1_ref, v2_ref, j2_ref, ts_ref, tp_ref, cand_ref, o1_ref, o2_ref, og_ref):
    h = _modulate(x_ref[...], sc_ref[...], sh_ref[...]).astype(BF16)
    hb_ref[...] = h
    hq = jnp.dot(h, wq_ref[...], preferred_element_type=F32).astype(BF16)
    tm = h.shape[0]
    nt = (((1,), (1,)), ((), ()))
    half = PEER_TOPK // 2
    cand_ref[_CAND_USED:, :] = jnp.full((N_CAND - _CAND_USED, tm), -jnp.inf, F32)
    for hd in range(PEER_HEADS):
        for p, (v_ref, j_ref) in enumerate(((v1_ref, j1_ref), (v2_ref, j2_ref))):
            qs = hq[:, (hd * 2 + p) * LANES:(hd * 2 + p + 1) * LANES]
            s = lax.dot_general(sk_ref[p], qs, nt, preferred_element_type=F32)
            for lb in range(tm // LANES):
                cols = slice(lb * LANES, (lb + 1) * LANES)
                _top16_keys(s[:, cols], v_ref, j_ref, cols)
        off = 0
        for a, n in _CAND_WIDE:
            cand_ref[off:off + n, :] = v1_ref[a:a + 1, :] + v2_ref[0:n, :]
            off += n
        cand_ref[_CAND_TAIL:_CAND_USED, :] = v1_ref[half:, :] + v2_ref[0:1, :]
        for lb in range(tm // LANES):
            cols = slice(lb * LANES, (lb + 1) * LANES)
            _top16_rows(cand_ref[:, cols], code_ref[:, cols], 2 * PEER_TOPK * PEER_TOPK, ts_ref, tp_ref, cols)
        ts = ts_ref[...]
        pos = tp_ref[...]
        ak = jnp.floor(pos * (1.0 / PEER_TOPK))
        bk = pos - PEER_TOPK * ak
        i1 = jnp.zeros_like(ts)
        i2 = jnp.zeros_like(ts)
        for a in range(PEER_TOPK):
            i1 = jnp.where(ak == a, j1_ref[a:a + 1, :], i1)
            i2 = jnp.where(bk == a, j2_ref[a:a + 1, :], i2)
        e = jnp.exp(ts - ts[0:1, :])
        rows = slice(hd * PEER_TOPK, (hd + 1) * PEER_TOPK)
        o1_ref[rows, :] = i1
        o2_ref[rows, :] = i2
        og_ref[rows, :] = e / jnp.sum(e, axis=0, keepdims=True)
    i1_ref[...] = o1_ref[...].T.astype(I32)
    i2_ref[...] = o2_ref[...].T.astype(I32)
    g_ref[...] = og_ref[...].T


def _peer_route(x, sc, sh, w_q, sub_keys, n_lat_tiles):
    rows, d = x.shape
    nq = w_q.shape[1]
    sel = lambda i: (i // n_lat_tiles, 0, 0)
    tile = lambda i: (i, 0)
    codes = jnp.broadcast_to(jnp.asarray(_cand_codes(), F32)[:, None], (N_CAND, TM))
    return pl.pallas_call(
        _peer_route_kernel,
        grid=(rows // TM,),
        in_specs=[pl.BlockSpec((TM, d), tile),
                  pl.BlockSpec((None, NB, d), sel),
                  pl.BlockSpec((None, NB, d), sel),
                  pl.BlockSpec((d, nq), lambda i: (0, 0)),
                  pl.BlockSpec(sub_keys.shape, lambda i: (0, 0, 0)),
                  pl.BlockSpec((N_CAND, TM), lambda i: (0, 0))],
        out_specs=[pl.BlockSpec((TM, d), tile),
                   pl.BlockSpec((TM, N_SLOTS), tile),
                   pl.BlockSpec((TM, N_SLOTS), tile),
                   pl.BlockSpec((TM, N_SLOTS), tile)],
        out_shape=[jax.ShapeDtypeStruct((rows, d), BF16),
                   jax.ShapeDtypeStruct((rows, N_SLOTS), I32),
                   jax.ShapeDtypeStruct((rows, N_SLOTS), I32),
                   jax.ShapeDtypeStruct((rows, N_SLOTS), F32)],
        scratch_shapes=[pltpu.VMEM((PEER_TOPK, TM), F32)] * 6
                       + [pltpu.VMEM((N_CAND, TM), F32)]
                       + [pltpu.VMEM((N_SLOTS, TM), F32)] * 3,
        compiler_params=_cparams(("arbitrary",)),
        name="peer_route",
    )(x, sc, sh, w_q, sub_keys, codes)


def _peer_act_kernel(n_steps, hb_ref, u_ref, i1_ref, i2_ref, g_ref, w_ref, z_ref, sa_ref, sb_ref):
    cj = pl.program_id(1)
    per_step = EXP_CHUNK // N_KEYS

    def scores(dst_ref):
        dst_ref[...] = lax.dot_general(hb_ref[...], u_ref[...], (((1,), (1,)), ((), ())),
                                       preferred_element_type=F32)

    def gather(src_ref):
        i1 = i1_ref[...]
        i2 = i2_ref[...]
        z = z_ref[...]
        for q in range(per_step):
            picked = jnp.take_along_axis(src_ref[:, q * N_KEYS:(q + 1) * N_KEYS], i2, axis=1)
            z = jnp.where(i1 == (cj - 1) * per_step + q, picked, z)
        z_ref[...] = z
        return z

    @pl.when(cj == 0)
    def _():
        z_ref[...] = jnp.zeros_like(z_ref)
        scores(sa_ref)

    steady = jnp.logical_and(cj > 0, cj < n_steps)

    @pl.when(jnp.logical_and(steady, cj % 2 == 1))
    def _():
        gather(sa_ref)
        scores(sb_ref)

    @pl.when(jnp.logical_and(steady, cj % 2 == 0))
    def _():
        gather(sb_ref)
        scores(sa_ref)

    @pl.when(cj == n_steps)
    def _():
        z = gather(sa_ref if n_steps % 2 == 1 else sb_ref)
        w_ref[...] = g_ref[...] * _gelu(z)


def _peer_act(hb, u_tab, i1, i2, g):
    rows, d = hb.shape
    n_exp = u_tab.shape[0]
    n_steps = n_exp // EXP_CHUNK
    tile = lambda i, j: (i, 0)
    return pl.pallas_call(
        functools.partial(_peer_act_kernel, n_steps),
        grid=(rows // TM_ACT, n_steps + 1),
        in_specs=[pl.BlockSpec((TM_ACT, d), tile),
                  pl.BlockSpec((EXP_CHUNK, d), lambda i, j: (jnp.minimum(j, n_steps - 1), 0)),
                  pl.BlockSpec((TM_ACT, N_SLOTS), tile),
                  pl.BlockSpec((TM_ACT, N_SLOTS), tile),
                  pl.BlockSpec((TM_ACT, N_SLOTS), tile)],
        out_specs=pl.BlockSpec((TM_ACT, N_SLOTS), tile),
        out_shape=jax.ShapeDtypeStruct((rows, N_SLOTS), F32),
        scratch_shapes=[pltpu.VMEM((TM_ACT, N_SLOTS), F32),
                        pltpu.VMEM((TM_ACT, EXP_CHUNK), F32),
                        pltpu.VMEM((TM_ACT, EXP_CHUNK), F32)],
        compiler_params=_cparams(("arbitrary", "arbitrary")),
        name="peer_act",
    )(hb, u_tab, i1, i2, g)


def _peer_out_kernel(alpha, i1_ref, i2_ref, w_ref, v_ref, x_ref, g2_ref, lg_ref, lb_ref, o_ref,
                     wbuf_ref, wd_ref, acc_ref):
    vj = pl.program_id(1)
    tm = x_ref.shape[0]
    groups = VAL_CHUNK // N_KEYS

    @pl.when(vj == 0)
    def _():
        acc_ref[...] = jnp.zeros_like(acc_ref)
        key_iota = lax.broadcasted_iota(I32, (N_KEYS, N_SLOTS), 0)
        nt = (((1,), (1,)), ((), ()))

        for sub in range(tm // ASM_TM):
            base = sub * ASM_TM

            def one_hots(t):
                row = pl.ds(base + t, 1)
                i1 = jnp.broadcast_to(i1_ref[row, :], (N_KEYS, N_SLOTS))
                i2 = jnp.broadcast_to(i2_ref[row, :], (N_KEYS, N_SLOTS))
                w = jnp.broadcast_to(w_ref[row, :], (N_KEYS, N_SLOTS))
                return (jnp.where(i1 == key_iota, w, 0.0).astype(BF16),
                        jnp.where(i2 == key_iota, 1.0, 0.0).astype(BF16))

            def assemble(tb, carry):
                for u in range(ASM_UNROLL):
                    t = tb * ASM_UNROLL + u
                    lhs, rhs = one_hots(t)
                    wbuf_ref[pl.ds(pl.multiple_of(t * W_STRIDE, 8), N_KEYS), :] = lax.dot_general(
                        lhs, rhs, nt, preferred_element_type=F32)
                return carry

            lax.fori_loop(0, ASM_TM // ASM_UNROLL, assemble, 0)

            def regroup(cb, carry):
                for u in range(ASM_UNROLL):
                    c = cb * ASM_UNROLL + u
                    wd_ref[c, base:base + ASM_TM, :] = wbuf_ref[pl.ds(c, ASM_TM, stride=W_STRIDE), :].astype(BF16)
                return carry

            lax.fori_loop(0, N_KEYS // ASM_UNROLL, regroup, 0)

    lhs = jnp.concatenate([wd_ref[vj * groups + q] for q in range(groups)], axis=1)
    acc_ref[...] += jnp.dot(lhs, v_ref[...], preferred_element_type=F32)

    @pl.when(vj == pl.num_programs(1) - 1)
    def _():
        o_ref[...] = _post_norm(x_ref[...], acc_ref[...], g2_ref[...], lg_ref[...], lb_ref[...], alpha)


def _peer_out(i1, i2, w, v_tab, x, g2, lg, lb, n_lat_tiles, alpha):
    rows, d = x.shape
    n_exp = v_tab.shape[0]
    tile = lambda i, j: (i, 0)
    const = lambda i, j: (0, 0)
    return pl.pallas_call(
        functools.partial(_peer_out_kernel, alpha),
        grid=(rows // TM_OUT, n_exp // VAL_CHUNK),
        in_specs=[pl.BlockSpec((TM_OUT, N_SLOTS), tile),
                  pl.BlockSpec((TM_OUT, N_SLOTS), tile),
                  pl.BlockSpec((TM_OUT, N_SLOTS), tile),
                  pl.BlockSpec((VAL_CHUNK, d), lambda i, j: (j, 0)),
                  pl.BlockSpec((TM_OUT, d), tile),
                  pl.BlockSpec((None, NB, d), lambda i, j: (i // n_lat_tiles, 0, 0)),
                  pl.BlockSpec((1, d), const),
                  pl.BlockSpec((1, d), const)],
        out_specs=pl.BlockSpec((TM_OUT, d), tile),
        out_shape=jax.ShapeDtypeStruct((rows, d), F32),
        scratch_shapes=[pltpu.VMEM((ASM_TM * W_STRIDE, N_KEYS), F32),
                        pltpu.VMEM((n_exp // N_KEYS, TM_OUT, N_KEYS), BF16),
                        pltpu.VMEM((TM_OUT, d), F32)],
        compiler_params=_cparams(("arbitrary", "arbitrary")),
        name="peer_out",
    )(i1, i2, w, v_tab, x, g2, lg, lb)


def kernel(x, c, ctx, c_ctx, mod_w, mod_b, ln_g, ln_b, lru_w_in, lru_conv_w, lru_conv_b, lru_gate_w, lru_gate_b,
           lru_lambda, lru_w_out, attn_w_qkv, attn_q_gain, attn_k_gain, attn_w_out, peer_w_q, peer_sub_keys,
           peer_u, peer_v):
    nb, seq, d = x.shape
    ctx_len = ctx.shape[1]
    depth = mod_w.shape[0]
    assert nb == NB and seq % GRID_W == 0 and seq % TQ == 0 and ctx_len % TQ == 0
    for t in (TM, TM_ACT, TM_OUT):
        assert (seq * NB) % t == 0 and (ctx_len * NB) % t == 0
    lt = seq + ctx_len
    n_lat, n_ctx = seq * NB // TM, ctx_len * NB // TM
    alpha = (2 * depth) ** 0.25

    xs = jnp.concatenate([x.transpose(1, 0, 2), ctx.transpose(1, 0, 2)], axis=0).reshape(lt * NB, d)

    cond = jnp.concatenate([c, c_ctx[None], jnp.zeros((7, d), F32)], axis=0)
    mtab = _modulation_table(cond, mod_w, mod_b)
    lat = mtab[:, :NB].reshape(depth, NB, 6, d).transpose(0, 2, 1, 3)
    cx = jnp.broadcast_to(mtab[:, NB].reshape(depth, 6, 1, d), (depth, 6, NB, d))
    mods = jnp.stack([lat, cx], axis=2)

    cos, sa, sb = _rope_tables(seq, ctx_len)
    lam = lru_lambda.reshape(-1, lru_lambda.shape[-1])
    softplus_neg_lam = pl.pallas_call(
        _softplus_kernel, out_shape=jax.ShapeDtypeStruct(lam.shape, F32), name="softplus")(-lam)
    softplus_neg_lam = softplus_neg_lam.reshape(lru_lambda.shape)

    for i in range(depth):
        sh1, sc1, g1, sh2, sc2, g2 = (mods[i, j] for j in range(6))
        lg1, lb1 = ln_g[i, 0][None], ln_b[i, 0][None]
        lg2, lb2 = ln_g[i, 1][None], ln_b[i, 1][None]
        j = i // 2
        if i % 2 == 0:
            gate, xr = _lru_in(xs, sc1, sh1, lru_w_in[j].astype(BF16), n_lat)
            wg0, gb0 = _gate_weights(lru_gate_w[j, 0], lru_gate_b[j, 0])
            wg1, gb1 = _gate_weights(lru_gate_w[j, 1], lru_gate_b[j, 1])
            xc, hf = _lru_fwd(xr, lru_conv_w[j], lru_conv_b[j][None], wg0, gb0, softplus_neg_lam[j, 0][None],
                              n_lat, n_ctx)
            xs = _lru_bwd(xc, hf, gate, xs, g1, wg1, gb1, softplus_neg_lam[j, 1][None],
                          lru_w_out[j].astype(BF16), lg1, lb1, n_lat, n_ctx, alpha)
        else:
            q, k, v = _qkv(xs, sc1, sh1, attn_w_qkv[j].astype(BF16), cos, sa, sb,
                           attn_q_gain[j][None], attn_k_gain[j][None], n_lat)
            need_ctx = i < depth - 1
            o = _attention(q, k, v, seq, need_ctx)
            xs = _attn_out(o, xs, g1, attn_w_out[j].astype(BF16), lg1, lb1, n_lat,
                           (lt if need_ctx else seq) * NB, alpha)
        hb, i1, i2, g = _peer_route(xs, sc2, sh2, peer_w_q[i].astype(BF16), peer_sub_keys[i].astype(BF16), n_lat)
        w = _peer_act(hb, peer_u[i].astype(BF16), i1, i2, g)
        xs = _peer_out(i1, i2, w, peer_v[i].astype(BF16), xs, g2, lg2, lb2, seq * NB // TM_OUT, alpha)

    return xs.reshape(-1, NB, d)[:seq].transpose(1, 0, 2)
```

```python
import functools

import jax
import jax.numpy as jnp
from jax import lax
from jax.experimental import pallas as pl
from jax.experimental.pallas import tpu as pltpu

F32 = jnp.float32
BF16 = jnp.bfloat16
I32 = jnp.int32

NB = 16
GRID_W = 64
CONV_W = 4
LRU_C = 8.0
N_Q_HEADS = 8
N_KV_HEADS = 2
HEAD_DIM = 128
Q_PER_KV = N_Q_HEADS // N_KV_HEADS
ROPE_THETA = 10000.0
PEER_HEADS = 8
N_KEYS = 128
PEER_TOPK = 16
N_SLOTS = PEER_HEADS * PEER_TOPK
LN_EPS = 1e-6
RMS_EPS = 1e-6
LANES = 128
VMEM_LIMIT = 56 * 1024 * 1024

TM = 256
TT = TM // NB
TQ = 256
TM_ACT = 1024
EXP_CHUNK = 1024
VAL_CHUNK = 4096
W_STRIDE = N_KEYS + 8
ASM_UNROLL = 64
TM_OUT = 512
ASM_TM = 64


def _cparams(sem):
    return pltpu.CompilerParams(dimension_semantics=sem, vmem_limit_bytes=VMEM_LIMIT)


def _gelu(x):
    return 0.5 * x * (1.0 + lax.erf(x * 0.7071067811865476))


def _modulate(x, sc, sh):
    rows, d = x.shape
    x3 = x.reshape(rows // NB, NB, d)
    return (x3 * (1.0 + sc)[None] + sh[None]).reshape(rows, d)


def _post_norm(x, delta, gate, lg, lb, alpha):
    rows, d = x.shape
    v = alpha * x + (delta.reshape(rows // NB, NB, d) * gate[None]).reshape(rows, d)
    mu = jnp.mean(v, axis=-1, keepdims=True)
    vc = v - mu
    var = jnp.mean(vc * vc, axis=-1, keepdims=True)
    return vc * lax.rsqrt(var + LN_EPS) * lg + lb


def _mod_kernel(c_ref, w_ref, b_ref, o_ref):
    c = c_ref[...]
    s = (c * jax.nn.sigmoid(c)).astype(BF16)
    o_ref[...] = jnp.dot(s, w_ref[...].astype(BF16), preferred_element_type=F32) + b_ref[...]


def _modulation_table(cond, mod_w, mod_b):
    depth, d, d6 = mod_w.shape
    nblk = d6 // d
    return pl.pallas_call(
        _mod_kernel,
        grid=(depth, nblk),
        in_specs=[pl.BlockSpec((cond.shape[0], d), lambda l, j: (0, 0)),
                  pl.BlockSpec((None, d, d), lambda l, j: (l, 0, j)),
                  pl.BlockSpec((None, 1, d), lambda l, j: (l, 0, j))],
        out_specs=pl.BlockSpec((None, cond.shape[0], d), lambda l, j: (l, 0, j)),
        out_shape=jax.ShapeDtypeStruct((depth, cond.shape[0], d6), F32),
        compiler_params=_cparams(("arbitrary", "arbitrary")),
        name="modulation",
    )(cond, mod_w, mod_b.reshape(depth, 1, d6))


def _lru_in_kernel(x_ref, sc_ref, sh_ref, w_ref, gate_ref, xr_ref):
    h = _modulate(x_ref[...], sc_ref[...], sh_ref[...]).astype(BF16)
    u = jnp.dot(h, w_ref[...], preferred_element_type=F32)
    c = gate_ref.shape[-1]
    gate_ref[...] = _gelu(u[:, :c])
    xr_ref[...] = u[:, c:]


def _lru_in(x, sc, sh, w_in, n_lat_tiles):
    rows, d = x.shape
    c2 = w_in.shape[1]
    c = c2 // 2
    sel = lambda i: (i // n_lat_tiles, 0, 0)
    return pl.pallas_call(
        _lru_in_kernel,
        grid=(rows // TM,),
        in_specs=[pl.BlockSpec((TM, d), lambda i: (i, 0)),
                  pl.BlockSpec((None, NB, d), sel),
                  pl.BlockSpec((None, NB, d), sel),
                  pl.BlockSpec((d, c2), lambda i: (0, 0))],
        out_specs=[pl.BlockSpec((TM, c), lambda i: (i, 0)),
                   pl.BlockSpec((TM, c), lambda i: (i, 0))],
        out_shape=[jax.ShapeDtypeStruct((rows, c), F32), jax.ShapeDtypeStruct((rows, c), F32)],
        compiler_params=_cparams(("arbitrary",)),
        name="lru_in",
    )(x, sc, sh, w_in)


def _row_permutation():
    r = jnp.arange(TM)
    src = (r % TT) * NB + r // TT
    return (src[:, None] == jnp.arange(TM)[None, :]).astype(BF16)


def _qkv_kernel(x_ref, sc_ref, sh_ref, perm_ref, w_ref, cos_ref, sa_ref, sb_ref, qg_ref, kg_ref,
                q_ref, k_ref, v_ref):
    h = _modulate(x_ref[...], sc_ref[...], sh_ref[...]).astype(BF16)
    hp = jnp.dot(perm_ref[...], h, preferred_element_type=F32).astype(BF16)
    qkv = jnp.dot(hp, w_ref[...], preferred_element_type=F32)
    nq = N_Q_HEADS * HEAD_DIM
    nk = N_KV_HEADS * HEAD_DIM
    per_row = lambda t_ref: jnp.broadcast_to(t_ref[...][None], (NB, TT, HEAD_DIM)).reshape(TM, HEAD_DIM)
    cos, sa, sb = per_row(cos_ref), per_row(sa_ref), per_row(sb_ref)

    def norm_rope(xh, gain):
        ms = jnp.mean(xh * xh, axis=-1, keepdims=True)
        xn = xh * lax.rsqrt(ms + RMS_EPS) * gain
        return xn * cos + pltpu.roll(xn, 96, axis=1) * sa + pltpu.roll(xn, 32, axis=1) * sb

    q = [norm_rope(qkv[:, h * HEAD_DIM:(h + 1) * HEAD_DIM], qg_ref[...]) for h in range(N_Q_HEADS)]
    k = [norm_rope(qkv[:, nq + h * HEAD_DIM:nq + (h + 1) * HEAD_DIM], kg_ref[...]) for h in range(N_KV_HEADS)]
    q_ref[...] = jnp.concatenate(q, axis=1).astype(BF16).reshape(NB, TT, nq)
    k_ref[...] = jnp.concatenate(k, axis=1).astype(BF16).reshape(NB, TT, nk)
    v_ref[...] = qkv[:, nq + nk:].astype(BF16).reshape(NB, TT, nk)


def _qkv(x, sc, sh, w, cos, sa, sb, qg, kg, n_lat_tiles):
    rows, d = x.shape
    lt = rows // NB
    nq = N_Q_HEADS * HEAD_DIM
    nk = N_KV_HEADS * HEAD_DIM
    sel = lambda i: (i // n_lat_tiles, 0, 0)
    const = lambda i: (0, 0)
    tab = pl.BlockSpec((TT, HEAD_DIM), lambda i: (i, 0))
    return pl.pallas_call(
        _qkv_kernel,
        grid=(rows // TM,),
        in_specs=[pl.BlockSpec((TM, d), lambda i: (i, 0)),
                  pl.BlockSpec((None, NB, d), sel),
                  pl.BlockSpec((None, NB, d), sel),
                  pl.BlockSpec((TM, TM), const),
                  pl.BlockSpec((d, nq + 2 * nk), const),
                  tab, tab, tab,
                  pl.BlockSpec((1, HEAD_DIM), const),
                  pl.BlockSpec((1, HEAD_DIM), const)],
        out_specs=[pl.BlockSpec((NB, TT, nq), lambda i: (0, i, 0)),
                   pl.BlockSpec((NB, TT, nk), lambda i: (0, i, 0)),
                   pl.BlockSpec((NB, TT, nk), lambda i: (0, i, 0))],
        out_shape=[jax.ShapeDtypeStruct((NB, lt, nq), BF16),
                   jax.ShapeDtypeStruct((NB, lt, nk), BF16),
                   jax.ShapeDtypeStruct((NB, lt, nk), BF16)],
        compiler_params=_cparams(("arbitrary",)),
        name="qkv",
    )(x, sc, sh, _row_permutation(), w, cos, sa, sb, qg, kg)


def _rglru_coeffs(xc, wg_ref, gb_ref, sp_ref, a_ref, b_ref):
    c = xc.shape[-1]
    groups, width, _ = wg_ref.shape
    xb = xc.astype(BF16)
    parts = [jnp.dot(xb[:, s * width:(s + 1) * width], wg_ref[s], preferred_element_type=F32)
             for s in range(groups)]
    r = jax.nn.sigmoid(jnp.concatenate([p[:, :width] for p in parts], axis=1) + gb_ref[:, :c])
    i = jax.nn.sigmoid(jnp.concatenate([p[:, width:] for p in parts], axis=1) + gb_ref[:, c:])
    log_a = -LRU_C * r * sp_ref[...]
    a = jnp.exp(log_a)
    a_ref[...] = a
    b_ref[...] = jnp.sqrt(-(a * a + 1.0) * jnp.tanh(log_a)) * i * xc


def _lru_fwd_kernel(n_lat, n_ctx, xr_ref, prev_ref, next_ref, cw_ref, cb_ref, wg_ref, gb_ref, sp_ref,
                    xc_ref, hf_ref, a_ref, b_ref, h_ref):
    g = pl.program_id(0)

    @pl.when(g == 0)
    def _():
        h_ref[...] = jnp.zeros_like(h_ref)

    first = jnp.logical_or(g == 0, g == n_ctx)
    last = jnp.logical_or(g == n_ctx - 1, g == n_ctx + n_lat - 1)
    prev = jnp.where(first, 0.0, prev_ref[...])
    nxt = jnp.where(last, 0.0, next_ref[...])
    ext = jnp.concatenate([prev, xr_ref[...], nxt], axis=0)
    cw = cw_ref[...]
    xc = cb_ref[...] + cw[0:1] * ext[0:TM]
    for j in range(1, CONV_W):
        xc = xc + cw[j:j + 1] * ext[j * NB:j * NB + TM]
    xc_ref[...] = xc
    _rglru_coeffs(xc, wg_ref, gb_ref, sp_ref, a_ref, b_ref)
    h = h_ref[...]
    for t in range(TT):
        rows = pl.ds(t * NB, NB)
        h = a_ref[rows, :] * h + b_ref[rows, :]
        hf_ref[rows, :] = h
    h_ref[...] = h


def _lru_fwd(xr, conv_w, conv_b, wg, gb, sp, n_lat, n_ctx):
    rows, c = xr.shape
    halo_p, halo_n = 2 * NB, NB
    per_p, per_n = TM // halo_p, TM // halo_n
    chunk = lambda g: jnp.where(g < n_ctx, n_lat + g, g - n_ctx)
    const = lambda g: (0, 0)
    return pl.pallas_call(
        functools.partial(_lru_fwd_kernel, n_lat, n_ctx),
        grid=(n_lat + n_ctx,),
        in_specs=[pl.BlockSpec((TM, c), lambda g: (chunk(g), 0)),
                  pl.BlockSpec((halo_p, c), lambda g: (jnp.maximum(chunk(g) * per_p - 1, 0), 0)),
                  pl.BlockSpec((halo_n, c), lambda g: (jnp.minimum((chunk(g) + 1) * per_n, rows // halo_n - 1), 0)),
                  pl.BlockSpec((CONV_W, c), const),
                  pl.BlockSpec((1, c), const),
                  pl.BlockSpec(wg.shape, lambda g: (0, 0, 0)),
                  pl.BlockSpec((1, 2 * c), const),
                  pl.BlockSpec((1, c), const)],
        out_specs=[pl.BlockSpec((TM, c), lambda g: (chunk(g), 0)),
                   pl.BlockSpec((TM, c), lambda g: (chunk(g), 0))],
        out_shape=[jax.ShapeDtypeStruct((rows, c), F32), jax.ShapeDtypeStruct((rows, c), F32)],
        scratch_shapes=[pltpu.VMEM((TM, c), F32), pltpu.VMEM((TM, c), F32), pltpu.VMEM((NB, c), F32)],
        compiler_params=_cparams(("arbitrary",)),
        name="lru_fwd",
    )(xr, xr, xr, conv_w, conv_b, wg, gb, sp)


def _lru_bwd_kernel(alpha, xc_ref, hf_ref, gate_ref, x_ref, g1_ref, wg_ref, gb_ref, sp_ref, wo_ref,
                    lg_ref, lb_ref, o_ref, a_ref, b_ref, y_ref, h_ref):
    g = pl.program_id(0)

    @pl.when(g == 0)
    def _():
        h_ref[...] = jnp.zeros_like(h_ref)

    _rglru_coeffs(xc_ref[...], wg_ref, gb_ref, sp_ref, a_ref, b_ref)
    h = h_ref[...]
    for t in range(TT - 1, -1, -1):
        rows = pl.ds(t * NB, NB)
        h = a_ref[rows, :] * h + b_ref[rows, :]
        y_ref[rows, :] = h
    h_ref[...] = h
    y = ((hf_ref[...] + y_ref[...]) * gate_ref[...]).astype(BF16)
    delta = jnp.dot(y, wo_ref[...], preferred_element_type=F32)
    o_ref[...] = _post_norm(x_ref[...], delta, g1_ref[...], lg_ref[...], lb_ref[...], alpha)


def _lru_bwd(xc, hf, gate, x, g1, wg, gb, sp, w_out, lg, lb, n_lat, n_ctx, alpha):
    rows, c = xc.shape
    d = x.shape[1]
    chunk = lambda g: jnp.where(g < n_ctx, n_lat + (n_ctx - 1 - g), n_lat - 1 - (g - n_ctx))
    tile = lambda g: (chunk(g), 0)
    const = lambda g: (0, 0)
    return pl.pallas_call(
        functools.partial(_lru_bwd_kernel, alpha),
        grid=(n_lat + n_ctx,),
        in_specs=[pl.BlockSpec((TM, c), tile),
                  pl.BlockSpec((TM, c), tile),
                  pl.BlockSpec((TM, c), tile),
                  pl.BlockSpec((TM, d), tile),
                  pl.BlockSpec((None, NB, d), lambda g: (jnp.where(g < n_ctx, 1, 0), 0, 0)),
                  pl.BlockSpec(wg.shape, lambda g: (0, 0, 0)),
                  pl.BlockSpec((1, 2 * c), const),
                  pl.BlockSpec((1, c), const),
                  pl.BlockSpec((c, d), const),
                  pl.BlockSpec((1, d), const),
                  pl.BlockSpec((1, d), const)],
        out_specs=pl.BlockSpec((TM, d), tile),
        out_shape=jax.ShapeDtypeStruct((rows, d), F32),
        scratch_shapes=[pltpu.VMEM((TM, c), F32), pltpu.VMEM((TM, c), F32), pltpu.VMEM((TM, c), F32),
                        pltpu.VMEM((NB, c), F32)],
        compiler_params=_cparams(("arbitrary",)),
        name="lru_bwd",
    )(xc, hf, gate, x, g1, wg, gb, sp, w_out, lg, lb)


def _softplus_kernel(x_ref, o_ref):
    x = x_ref[...]
    o_ref[...] = jax.nn.softplus(x)


def _gate_weights(gate_w, gate_b):
    nblk, blk, _ = gate_w.shape
    c = nblk * blk
    per = next(n for n in range(1, nblk + 1) if nblk % n == 0 and (n * blk) % LANES == 0)
    groups, width = nblk // per, per * blk
    eye = jnp.eye(per, dtype=gate_w.dtype)
    gw = gate_w.reshape(groups, per, blk, 2 * blk)
    wr = jnp.einsum('gnde,nm->gndme', gw[..., :blk], eye).reshape(groups, width, width)
    wi = jnp.einsum('gnde,nm->gndme', gw[..., blk:], eye).reshape(groups, width, width)
    w = jnp.concatenate([wr, wi], axis=2).astype(BF16)
    b = jnp.concatenate([gate_b[:, :blk].reshape(1, c), gate_b[:, blk:].reshape(1, c)], axis=1)
    return w, b


def _attn_kernel(n_lat_tiles, seq, q_ref, k_ref, v_ref, o_ref):
    qi = pl.program_id(2)
    tq = q_ref.shape[0]
    scale = HEAD_DIM ** -0.5

    def attend(k, v):
        for j in range(Q_PER_KV):
            cols = slice(j * HEAD_DIM, (j + 1) * HEAD_DIM)
            s = lax.dot_general(q_ref[:, cols], k, (((1,), (1,)), ((), ())), preferred_element_type=F32) * scale
            m = jnp.max(s, axis=-1, keepdims=True)
            p = jnp.exp(s - m)
            l = jnp.sum(p, axis=-1, keepdims=True)
            o = jnp.dot(p.astype(BF16), v, preferred_element_type=F32) / l
            o_ref[:, cols] = o.astype(BF16)

    @pl.when(qi < n_lat_tiles)
    def _():
        attend(k_ref[...], v_ref[...])

    @pl.when(qi >= n_lat_tiles)
    def _():
        attend(k_ref[seq:, :], v_ref[seq:, :])


def _attention(q, k, v, seq, need_ctx):
    nb, lt, nq = q.shape
    tq = TQ
    gw = Q_PER_KV * HEAD_DIM
    q_rows = lt if need_ctx else seq
    return pl.pallas_call(
        functools.partial(_attn_kernel, seq // tq, seq),
        grid=(nb, N_KV_HEADS, q_rows // tq),
        in_specs=[pl.BlockSpec((None, tq, gw), lambda b, h, i: (b, i, h)),
                  pl.BlockSpec((None, lt, HEAD_DIM), lambda b, h, i: (b, 0, h)),
                  pl.BlockSpec((None, lt, HEAD_DIM), lambda b, h, i: (b, 0, h))],
        out_specs=pl.BlockSpec((None, tq, gw), lambda b, h, i: (b, i, h)),
        out_shape=jax.ShapeDtypeStruct((nb, q_rows, nq), BF16),
        compiler_params=_cparams(("arbitrary", "arbitrary", "arbitrary")),
        name="attention",
    )(q, k, v)


def _attn_out_kernel(alpha, o_ref, x_ref, g1_ref, perm_ref, w_ref, lg_ref, lb_ref, out_ref):
    o_bt = o_ref[...].reshape(TM, o_ref.shape[-1])
    o_tb = jnp.dot(perm_ref[...], o_bt, preferred_element_type=F32).astype(BF16)
    delta = jnp.dot(o_tb, w_ref[...], preferred_element_type=F32)
    out_ref[...] = _post_norm(x_ref[...], delta, g1_ref[...], lg_ref[...], lb_ref[...], alpha)


def _attn_out(o, x, g1, w_out, lg, lb, n_lat_tiles, rows_out, alpha):
    d = x.shape[1]
    nq = w_out.shape[0]
    const = lambda i: (0, 0)
    return pl.pallas_call(
        functools.partial(_attn_out_kernel, alpha),
        grid=(rows_out // TM,),
        in_specs=[pl.BlockSpec((NB, TT, nq), lambda i: (0, i, 0)),
                  pl.BlockSpec((TM, d), lambda i: (i, 0)),
                  pl.BlockSpec((None, NB, d), lambda i: (i // n_lat_tiles, 0, 0)),
                  pl.BlockSpec((TM, TM), const),
                  pl.BlockSpec((nq, d), const),
                  pl.BlockSpec((1, d), const),
                  pl.BlockSpec((1, d), const)],
        out_specs=pl.BlockSpec((TM, d), lambda i: (i, 0)),
        out_shape=jax.ShapeDtypeStruct((rows_out, d), F32),
        compiler_params=_cparams(("arbitrary",)),
        name="attn_out",
    )(o, x, g1, _row_permutation().T, w_out, lg, lb)


def _rope_tables(seq, ctx_len):
    rows = seq // GRID_W
    row, col = jnp.meshgrid(jnp.arange(rows), jnp.arange(GRID_W), indexing='ij')
    pos = jnp.stack([row.reshape(-1), col.reshape(-1)], axis=-1).astype(F32)
    axis_dim = HEAD_DIM // 2
    inv_freq = ROPE_THETA ** (-jnp.arange(0, axis_dim, 2, dtype=F32) / axis_dim)
    ang = pos[:, :, None] * inv_freq
    cos, sin = jnp.cos(ang), jnp.sin(ang)
    zero = jnp.zeros_like(sin)
    cosf = jnp.concatenate([cos, cos], axis=-1).reshape(seq, HEAD_DIM)
    sa = jnp.concatenate([-sin, zero], axis=-1).reshape(seq, HEAD_DIM)
    sb = jnp.concatenate([zero, sin], axis=-1).reshape(seq, HEAD_DIM)
    pad = lambda t, v: jnp.concatenate([t, jnp.full((ctx_len, HEAD_DIM), v, F32)], axis=0)
    return pad(cosf, 1.0), pad(sa, 0.0), pad(sb, 0.0)


def _top16_rows(s, code, n, val_ref, idx_ref, cols):
    for r in range(PEER_TOPK):
        m = jnp.max(s, axis=0, keepdims=True)
        idx = jnp.min(jnp.where(s == m, code, float(n)), axis=0, keepdims=True)
        s = jnp.where(code == idx, -jnp.inf, s)
        val_ref[r:r + 1, cols] = m
        idx_ref[r:r + 1, cols] = idx


def _top16_keys(s, val_ref, idx_ref, cols):
    half = N_KEYS // 2
    a, b = s[:half], s[half:]
    row = lax.broadcasted_iota(I32, a.shape, 0).astype(F32)
    swap = b > a
    hi, lo = jnp.maximum(a, b), jnp.minimum(a, b)
    hi_idx = jnp.where(swap, row + half, row)
    lo_idx = jnp.where(swap, row, row + half)
    for r in range(PEER_TOPK):
        m = jnp.max(hi, axis=0, keepdims=True)
        idx = jnp.min(jnp.where(hi == m, hi_idx, float(N_KEYS)), axis=0, keepdims=True)
        taken = hi_idx == idx
        hi = jnp.where(taken, lo, hi)
        hi_idx = jnp.where(taken, lo_idx, hi_idx)
        lo = jnp.where(taken, -jnp.inf, lo)
        val_ref[r:r + 1, cols] = m
        idx_ref[r:r + 1, cols] = idx


_CAND_WIDE = [(a, PEER_TOPK // (a + 1)) for a in range(PEER_TOPK // 2)]
_CAND_TAIL = sum(n for _, n in _CAND_WIDE)
_CAND_USED = _CAND_TAIL + PEER_TOPK // 2
N_CAND = -(-_CAND_USED // 8) * 8


def _cand_codes():
    codes = [a * PEER_TOPK + b for a, n in _CAND_WIDE for b in range(n)]
    codes += [a * PEER_TOPK for a in range(PEER_TOPK // 2, PEER_TOPK)]
    codes += [PEER_TOPK * PEER_TOPK + i for i in range(N_CAND - len(codes))]
    return codes


def _peer_route_kernel(x_ref, sc_ref, sh_ref, wq_ref, sk_ref, code_ref, hb_ref, i1_ref, i2_ref, g_ref,
                       v1_ref, j1_ref, v2_ref, j2_ref, ts_ref, tp_ref, cand_ref, o1_ref, o2_ref, og_ref):
    h = _modulate(x_ref[...], sc_ref[...], sh_ref[...]).astype(BF16)
    hb_ref[...] = h
    hq = jnp.dot(h, wq_ref[...], preferred_element_type=F32).astype(BF16)
    tm = h.shape[0]
    nt = (((1,), (1,)), ((), ()))
    half = PEER_TOPK // 2
    cand_ref[_CAND_USED:, :] = jnp.full((N_CAND - _CAND_USED, tm), -jnp.inf, F32)
    for hd in range(PEER_HEADS):
        for p, (v_ref, j_ref) in enumerate(((v1_ref, j1_ref), (v2_ref, j2_ref))):
            qs = hq[:, (hd * 2 + p) * LANES:(hd * 2 + p + 1) * LANES]
            s = lax.dot_general(sk_ref[p], qs, nt, preferred_element_type=F32)
            for lb in range(tm // LANES):
                cols = slice(lb * LANES, (lb + 1) * LANES)
                _top16_keys(s[:, cols], v_ref, j_ref, cols)
        off = 0
        for a, n in _CAND_WIDE:
            cand_ref[off:off + n, :] = v1_ref[a:a + 1, :] + v2_ref[0:n, :]
            off += n
        cand_ref[_CAND_TAIL:_CAND_USED, :] = v1_ref[half:, :] + v2_ref[0:1, :]
        for lb in range(tm // LANES):
            cols = slice(lb * LANES, (lb + 1) * LANES)
            _top16_rows(cand_ref[:, cols], code_ref[:, cols], 2 * PEER_TOPK * PEER_TOPK, ts_ref, tp_ref, cols)
        ts = ts_ref[...]
        pos = tp_ref[...]
        ak = jnp.floor(pos * (1.0 / PEER_TOPK))
        bk = pos - PEER_TOPK * ak
        i1 = jnp.zeros_like(ts)
        i2 = jnp.zeros_like(ts)
        for a in range(PEER_TOPK):
            i1 = jnp.where(ak == a, j1_ref[a:a + 1, :], i1)
            i2 = jnp.where(bk == a, j2_ref[a:a + 1, :], i2)
        e = jnp.exp(ts - ts[0:1, :])
        rows = slice(hd * PEER_TOPK, (hd + 1) * PEER_TOPK)
        o1_ref[rows, :] = i1
        o2_ref[rows, :] = i2
        og_ref[rows, :] = e / jnp.sum(e, axis=0, keepdims=True)
    i1_ref[...] = o1_ref[...].T.astype(I32)
    i2_ref[...] = o2_ref[...].T.astype(I32)
    g_ref[...] = og_ref[...].T


def _peer_route(x, sc, sh, w_q, sub_keys, n_lat_tiles):
    rows, d = x.shape
    nq = w_q.shape[1]
    sel = lambda i: (i // n_lat_tiles, 0, 0)
    tile = lambda i: (i, 0)
    codes = jnp.broadcast_to(jnp.asarray(_cand_codes(), F32)[:, None], (N_CAND, TM))
    return pl.pallas_call(
        _peer_route_kernel,
        grid=(rows // TM,),
        in_specs=[pl.BlockSpec((TM, d), tile),
                  pl.BlockSpec((None, NB, d), sel),
                  pl.BlockSpec((None, NB, d), sel),
                  pl.BlockSpec((d, nq), lambda i: (0, 0)),
                  pl.BlockSpec(sub_keys.shape, lambda i: (0, 0, 0)),
                  pl.BlockSpec((N_CAND, TM), lambda i: (0, 0))],
        out_specs=[pl.BlockSpec((TM, d), tile),
                   pl.BlockSpec((TM, N_SLOTS), tile),
                   pl.BlockSpec((TM, N_SLOTS), tile),
                   pl.BlockSpec((TM, N_SLOTS), tile)],
        out_shape=[jax.ShapeDtypeStruct((rows, d), BF16),
                   jax.ShapeDtypeStruct((rows, N_SLOTS), I32),
                   jax.ShapeDtypeStruct((rows, N_SLOTS), I32),
                   jax.ShapeDtypeStruct((rows, N_SLOTS), F32)],
        scratch_shapes=[pltpu.VMEM((PEER_TOPK, TM), F32)] * 6
                       + [pltpu.VMEM((N_CAND, TM), F32)]
                       + [pltpu.VMEM((N_SLOTS, TM), F32)] * 3,
        compiler_params=_cparams(("arbitrary",)),
        name="peer_route",
    )(x, sc, sh, w_q, sub_keys, codes)


def _peer_act_kernel(n_steps, hb_ref, u_ref, i1_ref, i2_ref, g_ref, w_ref, z_ref, sa_ref, sb_ref):
    cj = pl.program_id(1)
    per_step = EXP_CHUNK // N_KEYS

    def scores(dst_ref):
        dst_ref[...] = lax.dot_general(hb_ref[...], u_ref[...], (((1,), (1,)), ((), ())),
                                       preferred_element_type=F32)

    def gather(src_ref):
        i1 = i1_ref[...]
        i2 = i2_ref[...]
        z = z_ref[...]
        for q in range(per_step):
            picked = jnp.take_along_axis(src_ref[:, q * N_KEYS:(q + 1) * N_KEYS], i2, axis=1)
            z = jnp.where(i1 == (cj - 1) * per_step + q, picked, z)
        z_ref[...] = z
        return z

    @pl.when(cj == 0)
    def _():
        z_ref[...] = jnp.zeros_like(z_ref)
        scores(sa_ref)

    steady = jnp.logical_and(cj > 0, cj < n_steps)

    @pl.when(jnp.logical_and(steady, cj % 2 == 1))
    def _():
        gather(sa_ref)
        scores(sb_ref)

    @pl.when(jnp.logical_and(steady, cj % 2 == 0))
    def _():
        gather(sb_ref)
        scores(sa_ref)

    @pl.when(cj == n_steps)
    def _():
        z = gather(sa_ref if n_steps % 2 == 1 else sb_ref)
        w_ref[...] = g_ref[...] * _gelu(z)


def _peer_act(hb, u_tab, i1, i2, g):
    rows, d = hb.shape
    n_exp = u_tab.shape[0]
    n_steps = n_exp // EXP_CHUNK
    tile = lambda i, j: (i, 0)
    return pl.pallas_call(
        functools.partial(_peer_act_kernel, n_steps),
        grid=(rows // TM_ACT, n_steps + 1),
        in_specs=[pl.BlockSpec((TM_ACT, d), tile),
                  pl.BlockSpec((EXP_CHUNK, d), lambda i, j: (jnp.minimum(j, n_steps - 1), 0)),
                  pl.BlockSpec((TM_ACT, N_SLOTS), tile),
                  pl.BlockSpec((TM_ACT, N_SLOTS), tile),
                  pl.BlockSpec((TM_ACT, N_SLOTS), tile)],
        out_specs=pl.BlockSpec((TM_ACT, N_SLOTS), tile),
        out_shape=jax.ShapeDtypeStruct((rows, N_SLOTS), F32),
        scratch_shapes=[pltpu.VMEM((TM_ACT, N_SLOTS), F32),
                        pltpu.VMEM((TM_ACT, EXP_CHUNK), F32),
                        pltpu.VMEM((TM_ACT, EXP_CHUNK), F32)],
        compiler_params=_cparams(("arbitrary", "arbitrary")),
        name="peer_act",
    )(hb, u_tab, i1, i2, g)


def _peer_out_kernel(alpha, i1_ref, i2_ref, w_ref, v_ref, x_ref, g2_ref, lg_ref, lb_ref, o_ref,
                     wbuf_ref, wd_ref, acc_ref):
    vj = pl.program_id(1)
    tm = x_ref.shape[0]
    groups = VAL_CHUNK // N_KEYS

    @pl.when(vj == 0)
    def _():
        acc_ref[...] = jnp.zeros_like(acc_ref)
        key_iota = lax.broadcasted_iota(I32, (N_KEYS, N_SLOTS), 0)
        nt = (((1,), (1,)), ((), ()))

        for sub in range(tm // ASM_TM):
            base = sub * ASM_TM

            def one_hots(t):
                row = pl.ds(base + t, 1)
                i1 = jnp.broadcast_to(i1_ref[row, :], (N_KEYS, N_SLOTS))
                i2 = jnp.broadcast_to(i2_ref[row, :], (N_KEYS, N_SLOTS))
                w = jnp.broadcast_to(w_ref[row, :], (N_KEYS, N_SLOTS))
                return (jnp.where(i1 == key_iota, w, 0.0).astype(BF16),
                        jnp.where(i2 == key_iota, 1.0, 0.0).astype(BF16))

            def assemble(tb, carry):
                for u in range(ASM_UNROLL):
                    t = tb * ASM_UNROLL + u
                    lhs, rhs = one_hots(t)
                    wbuf_ref[pl.ds(pl.multiple_of(t * W_STRIDE, 8), N_KEYS), :] = lax.dot_general(
                        lhs, rhs, nt, preferred_element_type=F32)
                return carry

            lax.fori_loop(0, ASM_TM // ASM_UNROLL, assemble, 0)

            def regroup(cb, carry):
                for u in range(ASM_UNROLL):
                    c = cb * ASM_UNROLL + u
                    wd_ref[c, base:base + ASM_TM, :] = wbuf_ref[pl.ds(c, ASM_TM, stride=W_STRIDE), :].astype(BF16)
                return carry

            lax.fori_loop(0, N_KEYS // ASM_UNROLL, regroup, 0)

    lhs = jnp.concatenate([wd_ref[vj * groups + q] for q in range(groups)], axis=1)
    acc_ref[...] += jnp.dot(lhs, v_ref[...], preferred_element_type=F32)

    @pl.when(vj == pl.num_programs(1) - 1)
    def _():
        o_ref[...] = _post_norm(x_ref[...], acc_ref[...], g2_ref[...], lg_ref[...], lb_ref[...], alpha)


def _peer_out(i1, i2, w, v_tab, x, g2, lg, lb, n_lat_tiles, alpha):
    rows, d = x.shape
    n_exp = v_tab.shape[0]
    tile = lambda i, j: (i, 0)
    const = lambda i, j: (0, 0)
    return pl.pallas_call(
        functools.partial(_peer_out_kernel, alpha),
        grid=(rows // TM_OUT, n_exp // VAL_CHUNK),
        in_specs=[pl.BlockSpec((TM_OUT, N_SLOTS), tile),
                  pl.BlockSpec((TM_OUT, N_SLOTS), tile),
                  pl.BlockSpec((TM_OUT, N_SLOTS), tile),
                  pl.BlockSpec((VAL_CHUNK, d), lambda i, j: (j, 0)),
                  pl.BlockSpec((TM_OUT, d), tile),
                  pl.BlockSpec((None, NB, d), lambda i, j: (i // n_lat_tiles, 0, 0)),
                  pl.BlockSpec((1, d), const),
                  pl.BlockSpec((1, d), const)],
        out_specs=pl.BlockSpec((TM_OUT, d), tile),
        out_shape=jax.ShapeDtypeStruct((rows, d), F32),
        scratch_shapes=[pltpu.VMEM((ASM_TM * W_STRIDE, N_KEYS), F32),
                        pltpu.VMEM((n_exp // N_KEYS, TM_OUT, N_KEYS), BF16),
                        pltpu.VMEM((TM_OUT, d), F32)],
        compiler_params=_cparams(("arbitrary", "arbitrary")),
        name="peer_out",
    )(i1, i2, w, v_tab, x, g2, lg, lb)


def kernel(x, c, ctx, c_ctx, mod_w, mod_b, ln_g, ln_b, lru_w_in, lru_conv_w, lru_conv_b, lru_gate_w, lru_gate_b,
           lru_lambda, lru_w_out, attn_w_qkv, attn_q_gain, attn_k_gain, attn_w_out, peer_w_q, peer_sub_keys,
           peer_u, peer_v):
    nb, seq, d = x.shape
    ctx_len = ctx.shape[1]
    depth = mod_w.shape[0]
    assert nb == NB and seq % GRID_W == 0 and seq % TQ == 0 and ctx_len % TQ == 0
    for t in (TM, TM_ACT, TM_OUT):
        assert (seq * NB) % t == 0 and (ctx_len * NB) % t == 0
    lt = seq + ctx_len
    n_lat, n_ctx = seq * NB // TM, ctx_len * NB // TM
    alpha = (2 * depth) ** 0.25

    xs = jnp.concatenate([x.transpose(1, 0, 2), ctx.transpose(1, 0, 2)], axis=0).reshape(lt * NB, d)

    cond = jnp.concatenate([c, c_ctx[None], jnp.zeros((7, d), F32)], axis=0)
    mtab = _modulation_table(cond, mod_w, mod_b)
    lat = mtab[:, :NB].reshape(depth, NB, 6, d).transpose(0, 2, 1, 3)
    cx = jnp.broadcast_to(mtab[:, NB].reshape(depth, 6, 1, d), (depth, 6, NB, d))
    mods = jnp.stack([lat, cx], axis=2)

    cos, sa, sb = _rope_tables(seq, ctx_len)
    lam = lru_lambda.reshape(-1, lru_lambda.shape[-1])
    softplus_neg_lam = pl.pallas_call(
        _softplus_kernel, out_shape=jax.ShapeDtypeStruct(lam.shape, F32), name="softplus")(-lam)
    softplus_neg_lam = softplus_neg_lam.reshape(lru_lambda.shape)

    for i in range(depth):
        sh1, sc1, g1, sh2, sc2, g2 = (mods[i, j] for j in range(6))
        lg1, lb1 = ln_g[i, 0][None], ln_b[i, 0][None]
        lg2, lb2 = ln_g[i, 1][None], ln_b[i, 1][None]
        j = i // 2
        if i % 2 == 0:
            gate, xr = _lru_in(xs, sc1, sh1, lru_w_in[j].astype(BF16), n_lat)
            wg0, gb0 = _gate_weights(lru_gate_w[j, 0], lru_gate_b[j, 0])
            wg1, gb1 = _gate_weights(lru_gate_w[j, 1], lru_gate_b[j, 1])
            xc, hf = _lru_fwd(xr, lru_conv_w[j], lru_conv_b[j][None], wg0, gb0, softplus_neg_lam[j, 0][None],
                              n_lat, n_ctx)
            xs = _lru_bwd(xc, hf, gate, xs, g1, wg1, gb1, softplus_neg_lam[j, 1][None],
                          lru_w_out[j].astype(BF16), lg1, lb1, n_lat, n_ctx, alpha)
        else:
            q, k, v = _qkv(xs, sc1, sh1, attn_w_qkv[j].astype(BF16), cos, sa, sb,
                           attn_q_gain[j][None], attn_k_gain[j][None], n_lat)
            need_ctx = i < depth - 1
            o = _attention(q, k, v, seq, need_ctx)
            xs = _attn_out(o, xs, g1, attn_w_out[j].astype(BF16), lg1, lb1, n_lat,
                           (lt if need_ctx else seq) * NB, alpha)
        hb, i1, i2, g = _peer_route(xs, sc2, sh2, peer_w_q[i].astype(BF16), peer_sub_keys[i].astype(BF16), n_lat)
        w = _peer_act(hb, peer_u[i].astype(BF16), i1, i2, g)
        xs = _peer_out(i1, i2, w, peer_v[i].astype(BF16), xs, g2, lg2, lb2, seq * NB // TM_OUT, alpha)

    return xs.reshape(-1, NB, d)[:seq].transpose(1, 0, 2)
```

```python
import functools

import jax
import jax.numpy as jnp
from jax import lax
from jax.experimental import pallas as pl
from jax.experimental.pallas import tpu as pltpu

F32 = jnp.float32
BF16 = jnp.bfloat16
I32 = jnp.int32

NB = 16
GRID_W = 64
CONV_W = 4
LRU_C = 8.0
N_Q_HEADS = 8
N_KV_HEADS = 2
HEAD_DIM = 128
Q_PER_KV = N_Q_HEADS // N_KV_HEADS
ROPE_THETA = 10000.0
PEER_HEADS = 8
N_KEYS = 128
PEER_TOPK = 16
N_SLOTS = PEER_HEADS * PEER_TOPK
LN_EPS = 1e-6
RMS_EPS = 1e-6
LANES = 128
VMEM_LIMIT = 56 * 1024 * 1024

TM = 256
TT = TM // NB
TQ = 256
TM_ACT = 1024
EXP_CHUNK = 1024
VAL_CHUNK = 4096
W_STRIDE = N_KEYS + 8
ASM_UNROLL = 128
TM_OUT = 512
ASM_TM = 128


def _cparams(sem):
    return pltpu.CompilerParams(dimension_semantics=sem, vmem_limit_bytes=VMEM_LIMIT)


def _gelu(x):
    return 0.5 * x * (1.0 + lax.erf(x * 0.7071067811865476))


def _modulate(x, sc, sh):
    rows, d = x.shape
    x3 = x.reshape(rows // NB, NB, d)
    return (x3 * (1.0 + sc)[None] + sh[None]).reshape(rows, d)


def _post_norm(x, delta, gate, lg, lb, alpha):
    rows, d = x.shape
    v = alpha * x + (delta.reshape(rows // NB, NB, d) * gate[None]).reshape(rows, d)
    mu = jnp.mean(v, axis=-1, keepdims=True)
    vc = v - mu
    var = jnp.mean(vc * vc, axis=-1, keepdims=True)
    return vc * lax.rsqrt(var + LN_EPS) * lg + lb


def _mod_kernel(c_ref, w_ref, b_ref, o_ref):
    c = c_ref[...]
    s = (c * jax.nn.sigmoid(c)).astype(BF16)
    o_ref[...] = jnp.dot(s, w_ref[...].astype(BF16), preferred_element_type=F32) + b_ref[...]


def _modulation_table(cond, mod_w, mod_b):
    depth, d, d6 = mod_w.shape
    nblk = d6 // d
    return pl.pallas_call(
        _mod_kernel,
        grid=(depth, nblk),
        in_specs=[pl.BlockSpec((cond.shape[0], d), lambda l, j: (0, 0)),
                  pl.BlockSpec((None, d, d), lambda l, j: (l, 0, j)),
                  pl.BlockSpec((None, 1, d), lambda l, j: (l, 0, j))],
        out_specs=pl.BlockSpec((None, cond.shape[0], d), lambda l, j: (l, 0, j)),
        out_shape=jax.ShapeDtypeStruct((depth, cond.shape[0], d6), F32),
        compiler_params=_cparams(("arbitrary", "arbitrary")),
        name="modulation",
    )(cond, mod_w, mod_b.reshape(depth, 1, d6))


def _lru_in_kernel(x_ref, sc_ref, sh_ref, w_ref, gate_ref, xr_ref):
    h = _modulate(x_ref[...], sc_ref[...], sh_ref[...]).astype(BF16)
    u = jnp.dot(h, w_ref[...], preferred_element_type=F32)
    c = gate_ref.shape[-1]
    gate_ref[...] = _gelu(u[:, :c])
    xr_ref[...] = u[:, c:]


def _lru_in(x, sc, sh, w_in, n_lat_tiles):
    rows, d = x.shape
    c2 = w_in.shape[1]
    c = c2 // 2
    sel = lambda i: (i // n_lat_tiles, 0, 0)
    return pl.pallas_call(
        _lru_in_kernel,
        grid=(rows // TM,),
        in_specs=[pl.BlockSpec((TM, d), lambda i: (i, 0)),
                  pl.BlockSpec((None, NB, d), sel),
                  pl.BlockSpec((None, NB, d), sel),
                  pl.BlockSpec((d, c2), lambda i: (0, 0))],
        out_specs=[pl.BlockSpec((TM, c), lambda i: (i, 0)),
                   pl.BlockSpec((TM, c), lambda i: (i, 0))],
        out_shape=[jax.ShapeDtypeStruct((rows, c), F32), jax.ShapeDtypeStruct((rows, c), F32)],
        compiler_params=_cparams(("arbitrary",)),
        name="lru_in",
    )(x, sc, sh, w_in)


def _row_permutation():
    r = jnp.arange(TM)
    src = (r % TT) * NB + r // TT
    return (src[:, None] == jnp.arange(TM)[None, :]).astype(BF16)


def _qkv_kernel(x_ref, sc_ref, sh_ref, perm_ref, w_ref, cos_ref, sa_ref, sb_ref, qg_ref, kg_ref,
                q_ref, k_ref, v_ref):
    h = _modulate(x_ref[...], sc_ref[...], sh_ref[...]).astype(BF16)
    hp = jnp.dot(perm_ref[...], h, preferred_element_type=F32).astype(BF16)
    qkv = jnp.dot(hp, w_ref[...], preferred_element_type=F32)
    nq = N_Q_HEADS * HEAD_DIM
    nk = N_KV_HEADS * HEAD_DIM
    per_row = lambda t_ref: jnp.broadcast_to(t_ref[...][None], (NB, TT, HEAD_DIM)).reshape(TM, HEAD_DIM)
    cos, sa, sb = per_row(cos_ref), per_row(sa_ref), per_row(sb_ref)

    def norm_rope(xh, gain):
        ms = jnp.mean(xh * xh, axis=-1, keepdims=True)
        xn = xh * lax.rsqrt(ms + RMS_EPS) * gain
        return xn * cos + pltpu.roll(xn, 96, axis=1) * sa + pltpu.roll(xn, 32, axis=1) * sb

    q = [norm_rope(qkv[:, h * HEAD_DIM:(h + 1) * HEAD_DIM], qg_ref[...]) for h in range(N_Q_HEADS)]
    k = [norm_rope(qkv[:, nq + h * HEAD_DIM:nq + (h + 1) * HEAD_DIM], kg_ref[...]) for h in range(N_KV_HEADS)]
    q_ref[...] = jnp.concatenate(q, axis=1).astype(BF16).reshape(NB, TT, nq)
    k_ref[...] = jnp.concatenate(k, axis=1).astype(BF16).reshape(NB, TT, nk)
    v_ref[...] = qkv[:, nq + nk:].astype(BF16).reshape(NB, TT, nk)


def _qkv(x, sc, sh, w, cos, sa, sb, qg, kg, n_lat_tiles):
    rows, d = x.shape
    lt = rows // NB
    nq = N_Q_HEADS * HEAD_DIM
    nk = N_KV_HEADS * HEAD_DIM
    sel = lambda i: (i // n_lat_tiles, 0, 0)
    const = lambda i: (0, 0)
    tab = pl.BlockSpec((TT, HEAD_DIM), lambda i: (i, 0))
    return pl.pallas_call(
        _qkv_kernel,
        grid=(rows // TM,),
        in_specs=[pl.BlockSpec((TM, d), lambda i: (i, 0)),
                  pl.BlockSpec((None, NB, d), sel),
                  pl.BlockSpec((None, NB, d), sel),
                  pl.BlockSpec((TM, TM), const),
                  pl.BlockSpec((d, nq + 2 * nk), const),
                  tab, tab, tab,
                  pl.BlockSpec((1, HEAD_DIM), const),
                  pl.BlockSpec((1, HEAD_DIM), const)],
        out_specs=[pl.BlockSpec((NB, TT, nq), lambda i: (0, i, 0)),
                   pl.BlockSpec((NB, TT, nk), lambda i: (0, i, 0)),
                   pl.BlockSpec((NB, TT, nk), lambda i: (0, i, 0))],
        out_shape=[jax.ShapeDtypeStruct((NB, lt, nq), BF16),
                   jax.ShapeDtypeStruct((NB, lt, nk), BF16),
                   jax.ShapeDtypeStruct((NB, lt, nk), BF16)],
        compiler_params=_cparams(("arbitrary",)),
        name="qkv",
    )(x, sc, sh, _row_permutation(), w, cos, sa, sb, qg, kg)


def _rglru_coeffs(xc, wg_ref, gb_ref, sp_ref, a_ref, b_ref):
    c = xc.shape[-1]
    groups, width, _ = wg_ref.shape
    xb = xc.astype(BF16)
    parts = [jnp.dot(xb[:, s * width:(s + 1) * width], wg_ref[s], preferred_element_type=F32)
             for s in range(groups)]
    r = jax.nn.sigmoid(jnp.concatenate([p[:, :width] for p in parts], axis=1) + gb_ref[:, :c])
    i = jax.nn.sigmoid(jnp.concatenate([p[:, width:] for p in parts], axis=1) + gb_ref[:, c:])
    log_a = -LRU_C * r * sp_ref[...]
    a = jnp.exp(log_a)
    a_ref[...] = a
    b_ref[...] = jnp.sqrt(-(a * a + 1.0) * jnp.tanh(log_a)) * i * xc


def _lru_fwd_kernel(n_lat, n_ctx, xr_ref, prev_ref, next_ref, cw_ref, cb_ref, wg_ref, gb_ref, sp_ref,
                    xc_ref, hf_ref, a_ref, b_ref, h_ref):
    g = pl.program_id(0)

    @pl.when(g == 0)
    def _():
        h_ref[...] = jnp.zeros_like(h_ref)

    first = jnp.logical_or(g == 0, g == n_ctx)
    last = jnp.logical_or(g == n_ctx - 1, g == n_ctx + n_lat - 1)
    prev = jnp.where(first, 0.0, prev_ref[...])
    nxt = jnp.where(last, 0.0, next_ref[...])
    ext = jnp.concatenate([prev, xr_ref[...], nxt], axis=0)
    cw = cw_ref[...]
    xc = cb_ref[...] + cw[0:1] * ext[0:TM]
    for j in range(1, CONV_W):
        xc = xc + cw[j:j + 1] * ext[j * NB:j * NB + TM]
    xc_ref[...] = xc
    _rglru_coeffs(xc, wg_ref, gb_ref, sp_ref, a_ref, b_ref)
    h = h_ref[...]
    for t in range(TT):
        rows = pl.ds(t * NB, NB)
        h = a_ref[rows, :] * h + b_ref[rows, :]
        hf_ref[rows, :] = h
    h_ref[...] = h


def _lru_fwd(xr, conv_w, conv_b, wg, gb, sp, n_lat, n_ctx):
    rows, c = xr.shape
    halo_p, halo_n = 2 * NB, NB
    per_p, per_n = TM // halo_p, TM // halo_n
    chunk = lambda g: jnp.where(g < n_ctx, n_lat + g, g - n_ctx)
    const = lambda g: (0, 0)
    return pl.pallas_call(
        functools.partial(_lru_fwd_kernel, n_lat, n_ctx),
        grid=(n_lat + n_ctx,),
        in_specs=[pl.BlockSpec((TM, c), lambda g: (chunk(g), 0)),
                  pl.BlockSpec((halo_p, c), lambda g: (jnp.maximum(chunk(g) * per_p - 1, 0), 0)),
                  pl.BlockSpec((halo_n, c), lambda g: (jnp.minimum((chunk(g) + 1) * per_n, rows // halo_n - 1), 0)),
                  pl.BlockSpec((CONV_W, c), const),
                  pl.BlockSpec((1, c), const),
                  pl.BlockSpec(wg.shape, lambda g: (0, 0, 0)),
                  pl.BlockSpec((1, 2 * c), const),
                  pl.BlockSpec((1, c), const)],
        out_specs=[pl.BlockSpec((TM, c), lambda g: (chunk(g), 0)),
                   pl.BlockSpec((TM, c), lambda g: (chunk(g), 0))],
        out_shape=[jax.ShapeDtypeStruct((rows, c), F32), jax.ShapeDtypeStruct((rows, c), F32)],
        scratch_shapes=[pltpu.VMEM((TM, c), F32), pltpu.VMEM((TM, c), F32), pltpu.VMEM((NB, c), F32)],
        compiler_params=_cparams(("arbitrary",)),
        name="lru_fwd",
    )(xr, xr, xr, conv_w, conv_b, wg, gb, sp)


def _lru_bwd_kernel(alpha, xc_ref, hf_ref, gate_ref, x_ref, g1_ref, wg_ref, gb_ref, sp_ref, wo_ref,
                    lg_ref, lb_ref, o_ref, a_ref, b_ref, y_ref, h_ref):
    g = pl.program_id(0)

    @pl.when(g == 0)
    def _():
        h_ref[...] = jnp.zeros_like(h_ref)

    _rglru_coeffs(xc_ref[...], wg_ref, gb_ref, sp_ref, a_ref, b_ref)
    h = h_ref[...]
    for t in range(TT - 1, -1, -1):
        rows = pl.ds(t * NB, NB)
        h = a_ref[rows, :] * h + b_ref[rows, :]
        y_ref[rows, :] = h
    h_ref[...] = h
    y = ((hf_ref[...] + y_ref[...]) * gate_ref[...]).astype(BF16)
    delta = jnp.dot(y, wo_ref[...], preferred_element_type=F32)
    o_ref[...] = _post_norm(x_ref[...], delta, g1_ref[...], lg_ref[...], lb_ref[...], alpha)


def _lru_bwd(xc, hf, gate, x, g1, wg, gb, sp, w_out, lg, lb, n_lat, n_ctx, alpha):
    rows, c = xc.shape
    d = x.shape[1]
    chunk = lambda g: jnp.where(g < n_ctx, n_lat + (n_ctx - 1 - g), n_lat - 1 - (g - n_ctx))
    tile = lambda g: (chunk(g), 0)
    const = lambda g: (0, 0)
    return pl.pallas_call(
        functools.partial(_lru_bwd_kernel, alpha),
        grid=(n_lat + n_ctx,),
        in_specs=[pl.BlockSpec((TM, c), tile),
                  pl.BlockSpec((TM, c), tile),
                  pl.BlockSpec((TM, c), tile),
                  pl.BlockSpec((TM, d), tile),
                  pl.BlockSpec((None, NB, d), lambda g: (jnp.where(g < n_ctx, 1, 0), 0, 0)),
                  pl.BlockSpec(wg.shape, lambda g: (0, 0, 0)),
                  pl.BlockSpec((1, 2 * c), const),
                  pl.BlockSpec((1, c), const),
                  pl.BlockSpec((c, d), const),
                  pl.BlockSpec((1, d), const),
                  pl.BlockSpec((1, d), const)],
        out_specs=pl.BlockSpec((TM, d), tile),
        out_shape=jax.ShapeDtypeStruct((rows, d), F32),
        scratch_shapes=[pltpu.VMEM((TM, c), F32), pltpu.VMEM((TM, c), F32), pltpu.VMEM((TM, c), F32),
                        pltpu.VMEM((NB, c), F32)],
        compiler_params=_cparams(("arbitrary",)),
        name="lru_bwd",
    )(xc, hf, gate, x, g1, wg, gb, sp, w_out, lg, lb)


def _softplus_kernel(x_ref, o_ref):
    x = x_ref[...]
    o_ref[...] = jax.nn.softplus(x)


def _gate_weights(gate_w, gate_b):
    nblk, blk, _ = gate_w.shape
    c = nblk * blk
    per = next(n for n in range(1, nblk + 1) if nblk % n == 0 and (n * blk) % LANES == 0)
    groups, width = nblk // per, per * blk
    eye = jnp.eye(per, dtype=gate_w.dtype)
    gw = gate_w.reshape(groups, per, blk, 2 * blk)
    wr = jnp.einsum('gnde,nm->gndme', gw[..., :blk], eye).reshape(groups, width, width)
    wi = jnp.einsum('gnde,nm->gndme', gw[..., blk:], eye).reshape(groups, width, width)
    w = jnp.concatenate([wr, wi], axis=2).astype(BF16)
    b = jnp.concatenate([gate_b[:, :blk].reshape(1, c), gate_b[:, blk:].reshape(1, c)], axis=1)
    return w, b


def _attn_kernel(n_lat_tiles, seq, q_ref, k_ref, v_ref, o_ref):
    qi = pl.program_id(2)
    tq = q_ref.shape[0]
    scale = HEAD_DIM ** -0.5

    def attend(k, v):
        for j in range(Q_PER_KV):
            cols = slice(j * HEAD_DIM, (j + 1) * HEAD_DIM)
            s = lax.dot_general(q_ref[:, cols], k, (((1,), (1,)), ((), ())), preferred_element_type=F32) * scale
            m = jnp.max(s, axis=-1, keepdims=True)
            p = jnp.exp(s - m)
            l = jnp.sum(p, axis=-1, keepdims=True)
            o = jnp.dot(p.astype(BF16), v, preferred_element_type=F32) / l
            o_ref[:, cols] = o.astype(BF16)

    @pl.when(qi < n_lat_tiles)
    def _():
        attend(k_ref[...], v_ref[...])

    @pl.when(qi >= n_lat_tiles)
    def _():
        attend(k_ref[seq:, :], v_ref[seq:, :])


def _attention(q, k, v, seq, need_ctx):
    nb, lt, nq = q.shape
    tq = TQ
    gw = Q_PER_KV * HEAD_DIM
    q_rows = lt if need_ctx else seq
    return pl.pallas_call(
        functools.partial(_attn_kernel, seq // tq, seq),
        grid=(nb, N_KV_HEADS, q_rows // tq),
        in_specs=[pl.BlockSpec((None, tq, gw), lambda b, h, i: (b, i, h)),
                  pl.BlockSpec((None, lt, HEAD_DIM), lambda b, h, i: (b, 0, h)),
                  pl.BlockSpec((None, lt, HEAD_DIM), lambda b, h, i: (b, 0, h))],
        out_specs=pl.BlockSpec((None, tq, gw), lambda b, h, i: (b, i, h)),
        out_shape=jax.ShapeDtypeStruct((nb, q_rows, nq), BF16),
        compiler_params=_cparams(("arbitrary", "arbitrary", "arbitrary")),
        name="attention",
    )(q, k, v)


def _attn_out_kernel(alpha, o_ref, x_ref, g1_ref, perm_ref, w_ref, lg_ref, lb_ref, out_ref):
    o_bt = o_ref[...].reshape(TM, o_ref.shape[-1])
    o_tb = jnp.dot(perm_ref[...], o_bt, preferred_element_type=F32).astype(BF16)
    delta = jnp.dot(o_tb, w_ref[...], preferred_element_type=F32)
    out_ref[...] = _post_norm(x_ref[...], delta, g1_ref[...], lg_ref[...], lb_ref[...], alpha)


def _attn_out(o, x, g1, w_out, lg, lb, n_lat_tiles, rows_out, alpha):
    d = x.shape[1]
    nq = w_out.shape[0]
    const = lambda i: (0, 0)
    return pl.pallas_call(
        functools.partial(_attn_out_kernel, alpha),
        grid=(rows_out // TM,),
        in_specs=[pl.BlockSpec((NB, TT, nq), lambda i: (0, i, 0)),
                  pl.BlockSpec((TM, d), lambda i: (i, 0)),
                  pl.BlockSpec((None, NB, d), lambda i: (i // n_lat_tiles, 0, 0)),
                  pl.BlockSpec((TM, TM), const),
                  pl.BlockSpec((nq, d), const),
                  pl.BlockSpec((1, d), const),
                  pl.BlockSpec((1, d), const)],
        out_specs=pl.BlockSpec((TM, d), lambda i: (i, 0)),
        out_shape=jax.ShapeDtypeStruct((rows_out, d), F32),
        compiler_params=_cparams(("arbitrary",)),
        name="attn_out",
    )(o, x, g1, _row_permutation().T, w_out, lg, lb)


def _rope_tables(seq, ctx_len):
    rows = seq // GRID_W
    row, col = jnp.meshgrid(jnp.arange(rows), jnp.arange(GRID_W), indexing='ij')
    pos = jnp.stack([row.reshape(-1), col.reshape(-1)], axis=-1).astype(F32)
    axis_dim = HEAD_DIM // 2
    inv_freq = ROPE_THETA ** (-jnp.arange(0, axis_dim, 2, dtype=F32) / axis_dim)
    ang = pos[:, :, None] * inv_freq
    cos, sin = jnp.cos(ang), jnp.sin(ang)
    zero = jnp.zeros_like(sin)
    cosf = jnp.concatenate([cos, cos], axis=-1).reshape(seq, HEAD_DIM)
    sa = jnp.concatenate([-sin, zero], axis=-1).reshape(seq, HEAD_DIM)
    sb = jnp.concatenate([zero, sin], axis=-1).reshape(seq, HEAD_DIM)
    pad = lambda t, v: jnp.concatenate([t, jnp.full((ctx_len, HEAD_DIM), v, F32)], axis=0)
    return pad(cosf, 1.0), pad(sa, 0.0), pad(sb, 0.0)


def _top16_rows(s, code, n, val_ref, idx_ref, cols):
    for r in range(PEER_TOPK):
        m = jnp.max(s, axis=0, keepdims=True)
        idx = jnp.min(jnp.where(s == m, code, float(n)), axis=0, keepdims=True)
        s = jnp.where(code == idx, -jnp.inf, s)
        val_ref[r:r + 1, cols] = m
        idx_ref[r:r + 1, cols] = idx


def _top16_keys(s, val_ref, idx_ref, cols):
    half = N_KEYS // 2
    a, b = s[:half], s[half:]
    row = lax.broadcasted_iota(I32, a.shape, 0).astype(F32)
    swap = b > a
    hi, lo = jnp.maximum(a, b), jnp.minimum(a, b)
    hi_idx = jnp.where(swap, row + half, row)
    lo_idx = jnp.where(swap, row, row + half)
    for r in range(PEER_TOPK):
        m = jnp.max(hi, axis=0, keepdims=True)
        idx = jnp.min(jnp.where(hi == m, hi_idx, float(N_KEYS)), axis=0, keepdims=True)
        taken = hi_idx == idx
        hi = jnp.where(taken, lo, hi)
        hi_idx = jnp.where(taken, lo_idx, hi_idx)
        lo = jnp.where(taken, -jnp.inf, lo)
        val_ref[r:r + 1, cols] = m
        idx_ref[r:r + 1, cols] = idx


_CAND_WIDE = [(a, PEER_TOPK // (a + 1)) for a in range(PEER_TOPK // 2)]
_CAND_TAIL = sum(n for _, n in _CAND_WIDE)
_CAND_USED = _CAND_TAIL + PEER_TOPK // 2
N_CAND = -(-_CAND_USED // 8) * 8


def _cand_codes():
    codes = [a * PEER_TOPK + b for a, n in _CAND_WIDE for b in range(n)]
    codes += [a * PEER_TOPK for a in range(PEER_TOPK // 2, PEER_TOPK)]
    codes += [PEER_TOPK * PEER_TOPK + i for i in range(N_CAND - len(codes))]
    return codes


def _peer_route_kernel(x_ref, sc_ref, sh_ref, wq_ref, sk_ref, code_ref, hb_ref, i1_ref, i2_ref, g_ref,
                       v1_ref, j1_ref, v2_ref, j2_ref, ts_ref, tp_ref, cand_ref, o1_ref, o2_ref, og_ref):
    h = _modulate(x_ref[...], sc_ref[...], sh_ref[...]).astype(BF16)
    hb_ref[...] = h
    hq = jnp.dot(h, wq_ref[...], preferred_element_type=F32).astype(BF16)
    tm = h.shape[0]
    nt = (((1,), (1,)), ((), ()))
    half = PEER_TOPK // 2
    cand_ref[_CAND_USED:, :] = jnp.full((N_CAND - _CAND_USED, tm), -jnp.inf, F32)
    for hd in range(PEER_HEADS):
        for p, (v_ref, j_ref) in enumerate(((v1_ref, j1_ref), (v2_ref, j2_ref))):
            qs = hq[:, (hd * 2 + p) * LANES:(hd * 2 + p + 1) * LANES]
            s = lax.dot_general(sk_ref[p], qs, nt, preferred_element_type=F32)
            for lb in range(tm // LANES):
                cols = slice(lb * LANES, (lb + 1) * LANES)
                _top16_keys(s[:, cols], v_ref, j_ref, cols)
        off = 0
        for a, n in _CAND_WIDE:
            cand_ref[off:off + n, :] = v1_ref[a:a + 1, :] + v2_ref[0:n, :]
            off += n
        cand_ref[_CAND_TAIL:_CAND_USED, :] = v1_ref[half:, :] + v2_ref[0:1, :]
        for lb in range(tm // LANES):
            cols = slice(lb * LANES, (lb + 1) * LANES)
            _top16_rows(cand_ref[:, cols], code_ref[:, cols], 2 * PEER_TOPK * PEER_TOPK, ts_ref, tp_ref, cols)
        ts = ts_ref[...]
        pos = tp_ref[...]
        ak = jnp.floor(pos * (1.0 / PEER_TOPK))
        bk = pos - PEER_TOPK * ak
        i1 = jnp.zeros_like(ts)
        i2 = jnp.zeros_like(ts)
        for a in range(PEER_TOPK):
            i1 = jnp.where(ak == a, j1_ref[a:a + 1, :], i1)
            i2 = jnp.where(bk == a, j2_ref[a:a + 1, :], i2)
        e = jnp.exp(ts - ts[0:1, :])
        rows = slice(hd * PEER_TOPK, (hd + 1) * PEER_TOPK)
        o1_ref[rows, :] = i1
        o2_ref[rows, :] = i2
        og_ref[rows, :] = e / jnp.sum(e, axis=0, keepdims=True)
    i1_ref[...] = o1_ref[...].T.astype(I32)
    i2_ref[...] = o2_ref[...].T.astype(I32)
    g_ref[...] = og_ref[...].T


def _peer_route(x, sc, sh, w_q, sub_keys, n_lat_tiles):
    rows, d = x.shape
    nq = w_q.shape[1]
    sel = lambda i: (i // n_lat_tiles, 0, 0)
    tile = lambda i: (i, 0)
    codes = jnp.broadcast_to(jnp.asarray(_cand_codes(), F32)[:, None], (N_CAND, TM))
    return pl.pallas_call(
        _peer_route_kernel,
        grid=(rows // TM,),
        in_specs=[pl.BlockSpec((TM, d), tile),
                  pl.BlockSpec((None, NB, d), sel),
                  pl.BlockSpec((None, NB, d), sel),
                  pl.BlockSpec((d, nq), lambda i: (0, 0)),
                  pl.BlockSpec(sub_keys.shape, lambda i: (0, 0, 0)),
                  pl.BlockSpec((N_CAND, TM), lambda i: (0, 0))],
        out_specs=[pl.BlockSpec((TM, d), tile),
                   pl.BlockSpec((TM, N_SLOTS), tile),
                   pl.BlockSpec((TM, N_SLOTS), tile),
                   pl.BlockSpec((TM, N_SLOTS), tile)],
        out_shape=[jax.ShapeDtypeStruct((rows, d), BF16),
                   jax.ShapeDtypeStruct((rows, N_SLOTS), I32),
                   jax.ShapeDtypeStruct((rows, N_SLOTS), I32),
                   jax.ShapeDtypeStruct((rows, N_SLOTS), F32)],
        scratch_shapes=[pltpu.VMEM((PEER_TOPK, TM), F32)] * 6
                       + [pltpu.VMEM((N_CAND, TM), F32)]
                       + [pltpu.VMEM((N_SLOTS, TM), F32)] * 3,
        compiler_params=_cparams(("arbitrary",)),
        name="peer_route",
    )(x, sc, sh, w_q, sub_keys, codes)


def _peer_act_kernel(n_steps, hb_ref, u_ref, i1_ref, i2_ref, g_ref, w_ref, z_ref, sa_ref, sb_ref):
    cj = pl.program_id(1)
    per_step = EXP_CHUNK // N_KEYS

    def scores(dst_ref):
        dst_ref[...] = lax.dot_general(hb_ref[...], u_ref[...], (((1,), (1,)), ((), ())),
                                       preferred_element_type=F32)

    def gather(src_ref):
        i1 = i1_ref[...]
        i2 = i2_ref[...]
        z = z_ref[...]
        for q in range(per_step):
            picked = jnp.take_along_axis(src_ref[:, q * N_KEYS:(q + 1) * N_KEYS], i2, axis=1)
            z = jnp.where(i1 == (cj - 1) * per_step + q, picked, z)
        z_ref[...] = z
        return z

    @pl.when(cj == 0)
    def _():
        z_ref[...] = jnp.zeros_like(z_ref)
        scores(sa_ref)

    steady = jnp.logical_and(cj > 0, cj < n_steps)

    @pl.when(jnp.logical_and(steady, cj % 2 == 1))
    def _():
        gather(sa_ref)
        scores(sb_ref)

    @pl.when(jnp.logical_and(steady, cj % 2 == 0))
    def _():
        gather(sb_ref)
        scores(sa_ref)

    @pl.when(cj == n_steps)
    def _():
        z = gather(sa_ref if n_steps % 2 == 1 else sb_ref)
        w_ref[...] = g_ref[...] * _gelu(z)


def _peer_act(hb, u_tab, i1, i2, g):
    rows, d = hb.shape
    n_exp = u_tab.shape[0]
    n_steps = n_exp // EXP_CHUNK
    tile = lambda i, j: (i, 0)
    return pl.pallas_call(
        functools.partial(_peer_act_kernel, n_steps),
        grid=(rows // TM_ACT, n_steps + 1),
        in_specs=[pl.BlockSpec((TM_ACT, d), tile),
                  pl.BlockSpec((EXP_CHUNK, d), lambda i, j: (jnp.minimum(j, n_steps - 1), 0)),
                  pl.BlockSpec((TM_ACT, N_SLOTS), tile),
                  pl.BlockSpec((TM_ACT, N_SLOTS), tile),
                  pl.BlockSpec((TM_ACT, N_SLOTS), tile)],
        out_specs=pl.BlockSpec((TM_ACT, N_SLOTS), tile),
        out_shape=jax.ShapeDtypeStruct((rows, N_SLOTS), F32),
        scratch_shapes=[pltpu.VMEM((TM_ACT, N_SLOTS), F32),
                        pltpu.VMEM((TM_ACT, EXP_CHUNK), F32),
                        pltpu.VMEM((TM_ACT, EXP_CHUNK), F32)],
        compiler_params=_cparams(("arbitrary", "arbitrary")),
        name="peer_act",
    )(hb, u_tab, i1, i2, g)


def _peer_out_kernel(alpha, i1_ref, i2_ref, w_ref, v_ref, x_ref, g2_ref, lg_ref, lb_ref, o_ref,
                     wbuf_ref, wd_ref, acc_ref):
    vj = pl.program_id(1)
    tm = x_ref.shape[0]
    groups = VAL_CHUNK // N_KEYS

    @pl.when(vj == 0)
    def _():
        acc_ref[...] = jnp.zeros_like(acc_ref)
        key_iota = lax.broadcasted_iota(I32, (N_KEYS, N_SLOTS), 0)
        nt = (((1,), (1,)), ((), ()))

        for sub in range(tm // ASM_TM):
            base = sub * ASM_TM

            def one_hots(t):
                row = pl.ds(base + t, 1)
                i1 = jnp.broadcast_to(i1_ref[row, :], (N_KEYS, N_SLOTS))
                i2 = jnp.broadcast_to(i2_ref[row, :], (N_KEYS, N_SLOTS))
                w = jnp.broadcast_to(w_ref[row, :], (N_KEYS, N_SLOTS))
                return (jnp.where(i1 == key_iota, w, 0.0).astype(BF16),
                        jnp.where(i2 == key_iota, 1.0, 0.0).astype(BF16))

            def assemble(tb, carry):
                for u in range(ASM_UNROLL):
                    t = tb * ASM_UNROLL + u
                    lhs, rhs = one_hots(t)
                    wbuf_ref[pl.ds(pl.multiple_of(t * W_STRIDE, 8), N_KEYS), :] = lax.dot_general(
                        lhs, rhs, nt, preferred_element_type=F32)
                return carry

            lax.fori_loop(0, ASM_TM // ASM_UNROLL, assemble, 0)

            def regroup(cb, carry):
                for u in range(ASM_UNROLL):
                    c = cb * ASM_UNROLL + u
                    wd_ref[c, base:base + ASM_TM, :] = wbuf_ref[pl.ds(c, ASM_TM, stride=W_STRIDE), :].astype(BF16)
                return carry

            lax.fori_loop(0, N_KEYS // ASM_UNROLL, regroup, 0)

    lhs = jnp.concatenate([wd_ref[vj * groups + q] for q in range(groups)], axis=1)
    acc_ref[...] += jnp.dot(lhs, v_ref[...], preferred_element_type=F32)

    @pl.when(vj == pl.num_programs(1) - 1)
    def _():
        o_ref[...] = _post_norm(x_ref[...], acc_ref[...], g2_ref[...], lg_ref[...], lb_ref[...], alpha)


def _peer_out(i1, i2, w, v_tab, x, g2, lg, lb, n_lat_tiles, alpha):
    rows, d = x.shape
    n_exp = v_tab.shape[0]
    tile = lambda i, j: (i, 0)
    const = lambda i, j: (0, 0)
    return pl.pallas_call(
        functools.partial(_peer_out_kernel, alpha),
        grid=(rows // TM_OUT, n_exp // VAL_CHUNK),
        in_specs=[pl.BlockSpec((TM_OUT, N_SLOTS), tile),
                  pl.BlockSpec((TM_OUT, N_SLOTS), tile),
                  pl.BlockSpec((TM_OUT, N_SLOTS), tile),
                  pl.BlockSpec((VAL_CHUNK, d), lambda i, j: (j, 0)),
                  pl.BlockSpec((TM_OUT, d), tile),
                  pl.BlockSpec((None, NB, d), lambda i, j: (i // n_lat_tiles, 0, 0)),
                  pl.BlockSpec((1, d), const),
                  pl.BlockSpec((1, d), const)],
        out_specs=pl.BlockSpec((TM_OUT, d), tile),
        out_shape=jax.ShapeDtypeStruct((rows, d), F32),
        scratch_shapes=[pltpu.VMEM((ASM_TM * W_STRIDE, N_KEYS), F32),
                        pltpu.VMEM((n_exp // N_KEYS, TM_OUT, N_KEYS), BF16),
                        pltpu.VMEM((TM_OUT, d), F32)],
        compiler_params=_cparams(("arbitrary", "arbitrary")),
        name="peer_out",
    )(i1, i2, w, v_tab, x, g2, lg, lb)


def kernel(x, c, ctx, c_ctx, mod_w, mod_b, ln_g, ln_b, lru_w_in, lru_conv_w, lru_conv_b, lru_gate_w, lru_gate_b,
           lru_lambda, lru_w_out, attn_w_qkv, attn_q_gain, attn_k_gain, attn_w_out, peer_w_q, peer_sub_keys,
           peer_u, peer_v):
    nb, seq, d = x.shape
    ctx_len = ctx.shape[1]
    depth = mod_w.shape[0]
    assert nb == NB and seq % GRID_W == 0 and seq % TQ == 0 and ctx_len % TQ == 0
    for t in (TM, TM_ACT, TM_OUT):
        assert (seq * NB) % t == 0 and (ctx_len * NB) % t == 0
    lt = seq + ctx_len
    n_lat, n_ctx = seq * NB // TM, ctx_len * NB // TM
    alpha = (2 * depth) ** 0.25

    xs = jnp.concatenate([x.transpose(1, 0, 2), ctx.transpose(1, 0, 2)], axis=0).reshape(lt * NB, d)

    cond = jnp.concatenate([c, c_ctx[None], jnp.zeros((7, d), F32)], axis=0)
    mtab = _modulation_table(cond, mod_w, mod_b)
    lat = mtab[:, :NB].reshape(depth, NB, 6, d).transpose(0, 2, 1, 3)
    cx = jnp.broadcast_to(mtab[:, NB].reshape(depth, 6, 1, d), (depth, 6, NB, d))
    mods = jnp.stack([lat, cx], axis=2)

    cos, sa, sb = _rope_tables(seq, ctx_len)
    lam = lru_lambda.reshape(-1, lru_lambda.shape[-1])
    softplus_neg_lam = pl.pallas_call(
        _softplus_kernel, out_shape=jax.ShapeDtypeStruct(lam.shape, F32), name="softplus")(-lam)
    softplus_neg_lam = softplus_neg_lam.reshape(lru_lambda.shape)

    for i in range(depth):
        sh1, sc1, g1, sh2, sc2, g2 = (mods[i, j] for j in range(6))
        lg1, lb1 = ln_g[i, 0][None], ln_b[i, 0][None]
        lg2, lb2 = ln_g[i, 1][None], ln_b[i, 1][None]
        j = i // 2
        if i % 2 == 0:
            gate, xr = _lru_in(xs, sc1, sh1, lru_w_in[j].astype(BF16), n_lat)
            wg0, gb0 = _gate_weights(lru_gate_w[j, 0], lru_gate_b[j, 0])
            wg1, gb1 = _gate_weights(lru_gate_w[j, 1], lru_gate_b[j, 1])
            xc, hf = _lru_fwd(xr, lru_conv_w[j], lru_conv_b[j][None], wg0, gb0, softplus_neg_lam[j, 0][None],
                              n_lat, n_ctx)
            xs = _lru_bwd(xc, hf, gate, xs, g1, wg1, gb1, softplus_neg_lam[j, 1][None],
                          lru_w_out[j].astype(BF16), lg1, lb1, n_lat, n_ctx, alpha)
        else:
            q, k, v = _qkv(xs, sc1, sh1, attn_w_qkv[j].astype(BF16), cos, sa, sb,
                           attn_q_gain[j][None], attn_k_gain[j][None], n_lat)
            need_ctx = i < depth - 1
            o = _attention(q, k, v, seq, need_ctx)
            xs = _attn_out(o, xs, g1, attn_w_out[j].astype(BF16), lg1, lb1, n_lat,
                           (lt if need_ctx else seq) * NB, alpha)
        hb, i1, i2, g = _peer_route(xs, sc2, sh2, peer_w_q[i].astype(BF16), peer_sub_keys[i].astype(BF16), n_lat)
        w = _peer_act(hb, peer_u[i].astype(BF16), i1, i2, g)
        xs = _peer_out(i1, i2, w, peer_v[i].astype(BF16), xs, g2, lg2, lb2, seq * NB // TM_OUT, alpha)

    return xs.reshape(-1, NB, d)[:seq].transpose(1, 0, 2)
```
